```python
import math
import jax, jax.numpy as jnp
from jax import lax
import numpy as np

D_MODEL = 1024
BATCH = 8
SEQ = 4096
DEPTH = 2

N_A_LAYERS = DEPTH // 2
N_B_LAYERS = DEPTH - N_A_LAYERS
D_SSM = D_MODEL
SSM_GROUP = 16
N_GROUPS = D_SSM // SSM_GROUP
STATE = 64
N_HEADS = 16
HEAD_DIM = 64
D_ATT = N_HEADS * HEAD_DIM
D_FF = 2816
CONV_W = 3
Q_BLOCK = 128
EPS = 1e-6
DT_MIN = 1e-3
DT_MAX = 1e-1

kernel_name = "yoco_s5_stickbreaking_convffn"


def rms_norm(x, g):
    xf = x.astype(jnp.float32)
    y = xf * lax.rsqrt(jnp.mean(xf * xf, axis=-1, keepdims=True) + EPS)
    return (y * g.astype(jnp.float32)).astype(x.dtype)


def s5_mixer(h, w_in, a_re, a_im, log_dt, b_re, b_im, c_re, c_im, d_skip, w_glu):
    bsz, seq, _ = h.shape
    f32 = jnp.float32
    u = (h @ w_in).astype(f32).reshape(bsz, seq, N_GROUPS, SSM_GROUP)
    a_re = a_re.astype(f32)
    a_im = a_im.astype(f32)
    dt = jnp.exp(log_dt.astype(f32))[:, None]
    mag = jnp.exp(a_re * dt)
    ab_re = mag * jnp.cos(a_im * dt)
    ab_im = mag * jnp.sin(a_im * dt)
    den = a_re * a_re + a_im * a_im
    f_re = ((ab_re - 1.0) * a_re + ab_im * a_im) / den
    f_im = (ab_im * a_re - (ab_re - 1.0) * a_im) / den
    b_re = b_re.astype(f32)
    b_im = b_im.astype(f32)
    bb_re = f_re[..., None] * b_re - f_im[..., None] * b_im
    bb_im = f_re[..., None] * b_im + f_im[..., None] * b_re
    bu_re = jnp.einsum('blgh,gph->blgp', u, bb_re)
    bu_im = jnp.einsum('blgh,gph->blgp', u, bb_im)
    shape_a = (1, seq, N_GROUPS, STATE)
    a_seq_re = jnp.broadcast_to(ab_re[None, None], shape_a)
    a_seq_im = jnp.broadcast_to(ab_im[None, None], shape_a)

    def combine(e1, e2):
        a1r, a1i, b1r, b1i = e1
        a2r, a2i, b2r, b2i = e2
        return (a2r * a1r - a2i * a1i,
                a2r * a1i + a2i * a1r,
                a2r * b1r - a2i * b1i + b2r,
                a2r * b1i + a2i * b1r + b2i)

    _, _, s_re, s_im = lax.associative_scan(
        combine, (a_seq_re, a_seq_im, bu_re, bu_im), axis=1)
    y = (jnp.einsum('blgp,ghp->blgh', s_re, c_re.astype(f32))
         - jnp.einsum('blgp,ghp->blgh', s_im, c_im.astype(f32))
         + d_skip.astype(f32) * u)
    y = jax.nn.gelu(y.reshape(bsz, seq, D_SSM)).astype(h.dtype)
    z = y @ w_glu
    return z[..., :D_MODEL] * jax.nn.sigmoid(z[..., D_MODEL:])


def stick_breaking_attention(h, w_q, k, v, w_o):
    bsz, seq, _ = h.shape
    scale = HEAD_DIM ** -0.5
    q = (h @ w_q).reshape(bsz, seq, N_HEADS, HEAD_DIM).transpose(0, 2, 1, 3) * scale
    outs = []
    for blk in range(seq // Q_BLOCK):
        t0 = blk * Q_BLOCK
        nk = t0 + Q_BLOCK
        qb = q[:, :, t0:nk]
        kb = k[:, :, :nk]
        vb = v[:, :, :nk]
        z = jnp.einsum('bhqd,bhkd->bhqk', qb, kb).astype(jnp.float32)
        t_idx = t0 + jnp.arange(Q_BLOCK)[:, None]
        s_idx = jnp.arange(nk)[None, :]
        causal = s_idx < t_idx
        log_beta = jax.nn.log_sigmoid(z)
        log_one_minus = jnp.where(causal, log_beta - z, 0.0)
        rem = lax.cumsum(log_one_minus, axis=3, reverse=True) - log_one_minus
        w = jnp.where(causal, jnp.exp(log_beta + rem), 0.0)
        outs.append(jnp.einsum('bhqk,bhkd->bhqd', w.astype(vb.dtype), vb))
    o = jnp.concatenate(outs, axis=2).transpose(0, 2, 1, 3).reshape(bsz, seq, D_ATT)
    return o @ w_o


def conv_ffn(h, w_up, conv_w, conv_b, w_down):
    gu = h @ w_up
    g = gu[..., :D_FF]
    u = gu[..., D_FF:]
    g = lax.conv_general_dilated(
        g, conv_w, window_strides=(1,), padding=[(CONV_W - 1, 0)],
        dimension_numbers=('NWC', 'WIO', 'NWC'), feature_group_count=D_FF) + conv_b
    return (jax.nn.silu(g) * u) @ w_down


def setup_inputs(seed: int = 0) -> dict:
    key = jax.random.key(seed)
    ks = jax.random.split(key, 24)
    f32 = jnp.float32
    nrm = lambda k, shape, s: jax.random.normal(k, shape, f32) * s
    n_idx = jnp.arange(STATE, dtype=f32)
    return {
        "x": jax.random.normal(ks[0], (BATCH, SEQ, D_MODEL), f32),
        "norm_mix": 1.0 + nrm(ks[1], (DEPTH, D_MODEL), 0.02),
        "norm_ffn": 1.0 + nrm(ks[2], (DEPTH, D_MODEL), 0.02),
        "norm_kv": 1.0 + nrm(ks[3], (D_MODEL,), 0.02),
        "norm_final": 1.0 + nrm(ks[4], (D_MODEL,), 0.02),
        "ssm_w_in": nrm(ks[5], (N_A_LAYERS, D_MODEL, D_SSM), D_MODEL ** -0.5),
        "ssm_a_re": -0.5 * jnp.exp(nrm(ks[6], (N_A_LAYERS, N_GROUPS, STATE), 0.05)),
        "ssm_a_im": math.pi * n_idx + nrm(ks[7], (N_A_LAYERS, N_GROUPS, STATE), 0.01),
        "ssm_log_dt": jax.random.uniform(ks[8], (N_A_LAYERS, N_GROUPS), f32,
                                         minval=math.log(DT_MIN), maxval=math.log(DT_MAX)),
        "ssm_b_re": nrm(ks[9], (N_A_LAYERS, N_GROUPS, STATE, SSM_GROUP), (2 * SSM_GROUP) ** -0.5),
        "ssm_b_im": nrm(ks[10], (N_A_LAYERS, N_GROUPS, STATE, SSM_GROUP), (2 * SSM_GROUP) ** -0.5),
        "ssm_c_re": nrm(ks[11], (N_A_LAYERS, N_GROUPS, SSM_GROUP, STATE), (2 * STATE) ** -0.5),
        "ssm_c_im": nrm(ks[12], (N_A_LAYERS, N_GROUPS, SSM_GROUP, STATE), (2 * STATE) ** -0.5),
        "ssm_d": nrm(ks[13], (N_A_LAYERS, N_GROUPS, SSM_GROUP), 1.0),
        "ssm_w_glu": nrm(ks[14], (N_A_LAYERS, D_SSM, 2 * D_MODEL), D_SSM ** -0.5),
        "kv_w": nrm(ks[15], (D_MODEL, 2 * D_ATT), D_MODEL ** -0.5),
        "attn_w_q": nrm(ks[16], (N_B_LAYERS, D_MODEL, D_ATT), D_MODEL ** -0.5),
        "attn_w_o": nrm(ks[17], (N_B_LAYERS, D_ATT, D_MODEL), D_ATT ** -0.5),
        "ffn_w_up": nrm(ks[18], (DEPTH, D_MODEL, 2 * D_FF), D_MODEL ** -0.5),
        "ffn_conv_w": nrm(ks[19], (DEPTH, CONV_W, 1, D_FF), CONV_W ** -0.5),
        "ffn_conv_b": nrm(ks[20], (DEPTH, D_FF), 0.01),
        "ffn_w_down": nrm(ks[21], (DEPTH, D_FF, D_MODEL), D_FF ** -0.5),
    }


def reference(x, norm_mix, norm_ffn, norm_kv, norm_final,
              ssm_w_in, ssm_a_re, ssm_a_im, ssm_log_dt, ssm_b_re, ssm_b_im,
              ssm_c_re, ssm_c_im, ssm_d, ssm_w_glu,
              kv_w, attn_w_q, attn_w_o,
              ffn_w_up, ffn_conv_w, ffn_conv_b, ffn_w_down):
    bsz, seq, _ = x.shape
    k_shared = None
    v_shared = None
    for layer in range(DEPTH):
        h = rms_norm(x, norm_mix[layer])
        if layer < N_A_LAYERS:
            x = x + s5_mixer(h, ssm_w_in[layer], ssm_a_re[layer], ssm_a_im[layer],
                             ssm_log_dt[layer], ssm_b_re[layer], ssm_b_im[layer],
                             ssm_c_re[layer], ssm_c_im[layer], ssm_d[layer],
                             ssm_w_glu[layer])
        else:
            j = layer - N_A_LAYERS
            x = x + stick_breaking_attention(h, attn_w_q[j], k_shared, v_shared, attn_w_o[j])
        x = x + conv_ffn(rms_norm(x, norm_ffn[layer]), ffn_w_up[layer],
                         ffn_conv_w[layer], ffn_conv_b[layer], ffn_w_down[layer])
        if layer == N_A_LAYERS - 1:
            kv = rms_norm(x, norm_kv) @ kv_w
            k_shared = kv[..., :D_ATT].reshape(bsz, seq, N_HEADS, HEAD_DIM).transpose(0, 2, 1, 3)
            v_shared = kv[..., D_ATT:].reshape(bsz, seq, N_HEADS, HEAD_DIM).transpose(0, 2, 1, 3)
    return rms_norm(x, norm_final)
```

```python
import functools
import math

import jax
import jax.numpy as jnp
from jax import lax
from jax.experimental import pallas as pl
from jax.experimental.pallas import tpu as pltpu

F32 = jnp.float32
BF16 = jnp.bfloat16

D_MODEL = 1024
N_GROUPS = 64
SSM_GROUP = 16
STATE = 64
N_HEADS = 16
HEAD_DIM = 64
D_FF = 2816
EPS = 1e-6

LANES = 128
SUBLANES = 8
N_LANE_BLOCKS = D_MODEL // LANES
GROUPS_PER_PAIR = 2
N_PAIRS = N_GROUPS // GROUPS_PER_PAIR
PAIRS_PER_LANE_BLOCK = N_PAIRS // N_LANE_BLOCKS

S5_TT = 64
FFN_TL = 512
FF_CHUNK = 256
ATT_BQ = 256
ATT_BK = 256
SCAN_PAIRS = 8
VMEM_LIMIT = 56 * 1024 * 1024


def _rms(x, g):
    ms = jnp.mean(x * x, axis=-1, keepdims=True)
    return x * lax.rsqrt(ms + EPS) * g


def _resident(shape):
    nd = len(shape)
    return pl.BlockSpec(shape, lambda *_: (0,) * nd, pipeline_mode=pl.Buffered(1))


def _s5_kernel(x_ref, nw_ref, win_ref, wb_ref, wc_ref, are_ref, aim_ref, d_ref, wglu_ref,
               o_ref, utb_scr, bre_scr, bim_scr, ytb_scr, yg_scr, sre_scr, sim_scr, *, batch, tt):
    rows = batch * tt

    @pl.when(pl.program_id(0) == 0)
    def _():
        sre_scr[...] = jnp.zeros_like(sre_scr)
        sim_scr[...] = jnp.zeros_like(sim_scr)

    xb = x_ref[...].reshape(rows, D_MODEL)
    h = _rms(xb, nw_ref[...]).astype(BF16)
    u = jnp.dot(h, win_ref[...], preferred_element_type=F32)

    for c in range(N_LANE_BLOCKS):
        for b in range(batch):
            utb_scr[c, pl.ds(b, tt, stride=batch), :] = u[b * tt:(b + 1) * tt, c * LANES:(c + 1) * LANES]

    def bproj(k, _):
        lhs = utb_scr[k // PAIRS_PER_LANE_BLOCK].astype(BF16)
        bu = jnp.dot(lhs, wb_ref[k], preferred_element_type=F32)
        bre_scr[k] = bu[:, :LANES]
        bim_scr[k] = bu[:, LANES:]
        return 0
    lax.fori_loop(0, N_PAIRS, bproj, 0)

    for kg in range(N_PAIRS // SCAN_PAIRS):
        ks = [kg * SCAN_PAIRS + j for j in range(SCAN_PAIRS)]
        ar = [are_ref[k] for k in ks]
        ai = [aim_ref[k] for k in ks]
        init = tuple(s for k in ks for s in (sre_scr[k], sim_scr[k]))

        def step(t, carry):
            row = pl.multiple_of(t * SUBLANES, SUBLANES)
            new = []
            for j, k in enumerate(ks):
                sr, si = carry[2 * j], carry[2 * j + 1]
                nr = ar[j] * sr - ai[j] * si + bre_scr[k, pl.ds(row, SUBLANES), :]
                ni = ar[j] * si + ai[j] * sr + bim_scr[k, pl.ds(row, SUBLANES), :]
                bre_scr[k, pl.ds(row, SUBLANES), :] = nr
                bim_scr[k, pl.ds(row, SUBLANES), :] = ni
                new += [nr, ni]
            return tuple(new)

        fin = lax.fori_loop(0, tt, step, init, unroll=2)
        for j, k in enumerate(ks):
            sre_scr[k] = fin[2 * j]
            sim_scr[k] = fin[2 * j + 1]

    for c in range(N_LANE_BLOCKS):
        parts = []
        for q in range(PAIRS_PER_LANE_BLOCK):
            k = c * PAIRS_PER_LANE_BLOCK + q
            parts += [bre_scr[k].astype(BF16), bim_scr[k].astype(BF16)]
        lhs = jnp.concatenate(parts, axis=1)
        y = jnp.dot(lhs, wc_ref[c], preferred_element_type=F32)
        y = y + d_ref[:, c * LANES:(c + 1) * LANES] * utb_scr[c]
        ytb_scr[c] = jax.nn.gelu(y)
        for b in range(batch):
            yg_scr[pl.ds(b * tt, tt), c * LANES:(c + 1) * LANES] = (
                ytb_scr[c, pl.ds(b, tt, stride=batch), :].astype(BF16))

    z = jnp.dot(yg_scr[...], wglu_ref[...], preferred_element_type=F32)
    out = x_ref[...].reshape(rows, D_MODEL) + z[:, :D_MODEL] * jax.nn.sigmoid(z[:, D_MODEL:])
    o_ref[...] = out.reshape(batch, tt, D_MODEL)


def _s5_tables(a_re, a_im, log_dt, b_re, b_im, c_re, c_im, batch):
    dt = jnp.exp(log_dt)[:, None]
    mag = jnp.exp(a_re * dt)
    ab_re = mag * jnp.cos(a_im * dt)
    ab_im = mag * jnp.sin(a_im * dt)
    den = a_re * a_re + a_im * a_im
    f_re = ((ab_re - 1.0) * a_re + ab_im * a_im) / den
    f_im = (ab_im * a_re - (ab_re - 1.0) * a_im) / den
    bb_re = f_re[..., None] * b_re - f_im[..., None] * b_im
    bb_im = f_re[..., None] * b_im + f_im[..., None] * b_re
    groups_per_block = N_GROUPS // N_LANE_BLOCKS
    bb = jnp.stack([bb_re, bb_im]).reshape(2, N_PAIRS, GROUPS_PER_PAIR, STATE, SSM_GROUP)
    pair = jnp.arange(N_PAIRS)[:, None]
    member = jnp.arange(GROUPS_PER_PAIR)[None, :]
    local_group = (GROUPS_PER_PAIR * pair) % groups_per_block + member
    onehot = (local_group[..., None] == jnp.arange(groups_per_block)).astype(F32)
    wb = jnp.einsum('rkjph,kjg->kghrjp', bb, onehot).reshape(N_PAIRS, LANES, 2 * LANES)
    cc = jnp.stack([c_re, -c_im]).reshape(
        2, N_LANE_BLOCKS, PAIRS_PER_LANE_BLOCK, GROUPS_PER_PAIR, SSM_GROUP, STATE)
    wc = jnp.einsum('rcqjhp,qa,jb->cqrjpabh', cc, jnp.eye(PAIRS_PER_LANE_BLOCK, dtype=F32),
                    jnp.eye(GROUPS_PER_PAIR, dtype=F32)).reshape(N_LANE_BLOCKS, D_MODEL, LANES)
    a_re_t = jnp.broadcast_to(ab_re.reshape(N_PAIRS, 1, LANES), (N_PAIRS, batch, LANES))
    a_im_t = jnp.broadcast_to(ab_im.reshape(N_PAIRS, 1, LANES), (N_PAIRS, batch, LANES))
    return wb.astype(BF16), wc.astype(BF16), a_re_t, a_im_t


def _s5_mixer(x, norm_w, w_in, a_re, a_im, log_dt, b_re, b_im, c_re, c_im, d_skip, w_glu):
    batch, seq, _ = x.shape
    assert batch == SUBLANES, "the scan keeps one timestep of all batches in one sublane tile"
    tt = S5_TT
    rows = batch * tt
    wb, wc, a_re_t, a_im_t = _s5_tables(a_re, a_im, log_dt, b_re, b_im, c_re, c_im, batch)
    kern = functools.partial(_s5_kernel, batch=batch, tt=tt)
    return pl.pallas_call(
        kern,
        out_shape=jax.ShapeDtypeStruct(x.shape, F32),
        grid=(seq // tt,),
        in_specs=[
            pl.BlockSpec((batch, tt, D_MODEL), lambda i: (0, i, 0)),
            _resident((1, D_MODEL)),
            _resident((D_MODEL, D_MODEL)),
            _resident((N_PAIRS, LANES, 2 * LANES)),
            _resident((N_LANE_BLOCKS, D_MODEL, LANES)),
            _resident((N_PAIRS, batch, LANES)),
            _resident((N_PAIRS, batch, LANES)),
            _resident((1, D_MODEL)),
            _resident((D_MODEL, 2 * D_MODEL)),
        ],
        out_specs=pl.BlockSpec((batch, tt, D_MODEL), lambda i: (0, i, 0)),
        scratch_shapes=[
            pltpu.VMEM((N_LANE_BLOCKS, rows, LANES), F32),
            pltpu.VMEM((N_PAIRS, rows, LANES), F32),
            pltpu.VMEM((N_PAIRS, rows, LANES), F32),
            pltpu.VMEM((N_LANE_BLOCKS, rows, LANES), F32),
            pltpu.VMEM((rows, D_MODEL), BF16),
            pltpu.VMEM((N_PAIRS, batch, LANES), F32),
            pltpu.VMEM((N_PAIRS, batch, LANES), F32),
        ],
        compiler_params=pltpu.CompilerParams(
            dimension_semantics=("arbitrary",), vmem_limit_bytes=VMEM_LIMIT),
        name="s5_mixer",
    )(x, norm_w.reshape(1, D_MODEL), w_in.astype(BF16), wb, wc, a_re_t, a_im_t,
      d_skip.reshape(1, D_MODEL), w_glu.astype(BF16))


def _ffn_kernel(*refs, tl, pre_proj, final_norm):
    refs = list(refs)
    x_ref = refs.pop(0)
    if pre_proj:
        a_ref, wo_ref = refs.pop(0), refs.pop(0)
    nw_ref, wup_ref, cw_ref, cb_ref, wd_ref = (refs.pop(0) for _ in range(5))
    if final_norm:
        nf_ref = refs.pop(0)
    o_ref, gs_scr, act_scr = refs

    @pl.when(pl.program_id(1) == 0)
    def _():
        gs_scr[0:SUBLANES, :] = jnp.zeros((SUBLANES, D_FF), F32)

    x = x_ref[...]
    if pre_proj:
        x = x + jnp.dot(a_ref[...], wo_ref[...], preferred_element_type=F32)
    h = _rms(x, nw_ref[...]).astype(BF16)
    for c in range(D_FF // FF_CHUNK):
        sl = slice(c * FF_CHUNK, (c + 1) * FF_CHUNK)
        g = jnp.dot(h, wup_ref[:, sl], preferred_element_type=F32)
        up = jnp.dot(h, wup_ref[:, D_FF + c * FF_CHUNK:D_FF + (c + 1) * FF_CHUNK],
                     preferred_element_type=F32)
        gs_scr[SUBLANES:SUBLANES + tl, sl] = g
        g1 = gs_scr[SUBLANES - 1:SUBLANES - 1 + tl, sl]
        g2 = gs_scr[SUBLANES - 2:SUBLANES - 2 + tl, sl]
        gc = cw_ref[0:1, sl] * g2 + cw_ref[1:2, sl] * g1 + cw_ref[2:3, sl] * g + cb_ref[:, sl]
        act_scr[:, sl] = (gc * jax.nn.sigmoid(gc) * up).astype(BF16)
        gs_scr[0:SUBLANES, sl] = g[tl - SUBLANES:tl]
    y = x + jnp.dot(act_scr[...], wd_ref[...], preferred_element_type=F32)
    if final_norm:
        y = _rms(y, nf_ref[...])
    o_ref[...] = y


def _conv_ffn(x, norm_w, w_up, conv_w, conv_b, w_down, attn=None, w_o=None, final_w=None):
    batch, seq, _ = x.shape
    tl = FFN_TL
    pre_proj = attn is not None
    final_norm = final_w is not None
    row_spec = pl.BlockSpec((None, tl, D_MODEL), lambda b, i: (b, i, 0))
    in_specs = [row_spec]
    args = [x]
    if pre_proj:
        in_specs += [row_spec, _resident((D_MODEL, D_MODEL))]
        args += [attn, w_o.astype(BF16)]
    in_specs += [_resident((1, D_MODEL)), _resident((D_MODEL, 2 * D_FF)), _resident((3, D_FF)),
                 _resident((1, D_FF)), _resident((D_FF, D_MODEL))]
    args += [norm_w.reshape(1, D_MODEL), w_up.astype(BF16), conv_w.reshape(3, D_FF),
             conv_b.reshape(1, D_FF), w_down.astype(BF16)]
    if final_norm:
        in_specs.append(_resident((1, D_MODEL)))
        args.append(final_w.reshape(1, D_MODEL))
    kern = functools.partial(_ffn_kernel, tl=tl, pre_proj=pre_proj, final_norm=final_norm)
    return pl.pallas_call(
        kern,
        out_shape=jax.ShapeDtypeStruct(x.shape, F32),
        grid=(batch, seq // tl),
        in_specs=in_specs,
        out_specs=row_spec,
        scratch_shapes=[
            pltpu.VMEM((SUBLANES + tl, D_FF), F32),
            pltpu.VMEM((tl, D_FF), BF16),
        ],
        compiler_params=pltpu.CompilerParams(
            dimension_semantics=("arbitrary", "arbitrary"), vmem_limit_bytes=VMEM_LIMIT),
        name="conv_ffn",
    )(*args)


def _qkv_kernel(x_ref, nq_ref, nkv_ref, wq_ref, wkv_ref, q_ref, k_ref, v_ref):
    x = x_ref[...]
    hq = _rms(x, nq_ref[...]).astype(BF16)
    hkv = _rms(x, nkv_ref[...]).astype(BF16)
    q = jnp.dot(hq, wq_ref[...], preferred_element_type=F32) * (HEAD_DIM ** -0.5)
    kv = jnp.dot(hkv, wkv_ref[...], preferred_element_type=F32)
    q_ref[...] = q.astype(BF16)
    k_ref[...] = kv[:, :D_MODEL].astype(BF16)
    v_ref[...] = kv[:, D_MODEL:].astype(BF16)


def _qkv_proj(x, norm_q, norm_kv, w_q, w_kv):
    batch, seq, _ = x.shape
    tl = FFN_TL
    row_spec = pl.BlockSpec((None, tl, D_MODEL), lambda b, i: (b, i, 0))
    out = jax.ShapeDtypeStruct(x.shape, BF16)
    return pl.pallas_call(
        _qkv_kernel,
        out_shape=(out, out, out),
        grid=(batch, seq // tl),
        in_specs=[row_spec, _resident((1, D_MODEL)), _resident((1, D_MODEL)),
                  _resident((D_MODEL, D_MODEL)), _resident((D_MODEL, 2 * D_MODEL))],
        out_specs=(row_spec, row_spec, row_spec),
        compiler_params=pltpu.CompilerParams(
            dimension_semantics=("arbitrary", "arbitrary"), vmem_limit_bytes=VMEM_LIMIT),
        name="qkv_proj",
    )(x, norm_q.reshape(1, D_MODEL), norm_kv.reshape(1, D_MODEL), w_q.astype(BF16),
      w_kv.astype(BF16))


def _attn_kernel(q_ref, k_ref, v_ref, tri_ref, o_ref, acc_scr, rem_scr, *, bq, bk):
    qi = pl.program_id(2)
    q2 = q_ref[...]
    lane = lax.broadcasted_iota(jnp.int32, (bq, LANES), 1)
    first = lane < HEAD_DIM
    zero = jnp.zeros_like(q2)
    q_heads = (jnp.where(first, q2, zero), jnp.where(first, zero, q2))
    row = lax.broadcasted_iota(jnp.int32, (bq, bk), 0)
    col = lax.broadcasted_iota(jnp.int32, (bq, bk), 1)
    causal = col < row

    acc_scr[...] = jnp.zeros_like(acc_scr)
    rem_scr[...] = jnp.zeros_like(rem_scr)

    def block(j, diag):
        start = pl.multiple_of(j * bk, bk)
        kb = k_ref[pl.ds(start, bk), :]
        vb = v_ref[pl.ds(start, bk), :]
        for hd in range(2):
            z = lax.dot_general(q_heads[hd], kb, (((1,), (1,)), ((), ())),
                                preferred_element_type=F32)
            sp = jnp.maximum(z, 0.0) + jnp.log(1.0 + jnp.exp(-jnp.abs(z)))
            if diag:
                sp = jnp.where(causal, sp, 0.0)
            cs = jnp.dot(sp.astype(BF16), tri_ref[...], preferred_element_type=F32)
            rem = rem_scr[hd]
            w = jnp.exp(z - cs - jnp.concatenate([rem] * (bk // LANES), axis=1))
            if diag:
                w = jnp.where(causal, w, 0.0)
            acc_scr[hd] += jnp.dot(w.astype(BF16), vb, preferred_element_type=F32)
            rem_scr[hd] = rem + jnp.sum(sp, axis=-1, keepdims=True)

    block(qi, True)

    def body(it, _):
        block(qi - 1 - it, False)
        return 0
    lax.fori_loop(0, qi, body, 0)

    o_ref[...] = jnp.where(first, acc_scr[0], acc_scr[1]).astype(BF16)


def _sb_attention(q, k, v):
    batch, seq, _ = q.shape
    bq, bk = ATT_BQ, ATT_BK
    assert bq == bk
    tri = (jnp.arange(bk)[:, None] >= jnp.arange(bk)[None, :]).astype(BF16)
    kern = functools.partial(_attn_kernel, bq=bq, bk=bk)
    kv_spec = pl.BlockSpec((None, seq, LANES), lambda b, hp, i: (b, 0, hp))
    q_spec = pl.BlockSpec((None, bq, LANES), lambda b, hp, i: (b, i, hp))
    return pl.pallas_call(
        kern,
        out_shape=jax.ShapeDtypeStruct(q.shape, BF16),
        grid=(batch, N_HEADS * HEAD_DIM // LANES, seq // bq),
        in_specs=[q_spec, kv_spec, kv_spec, _resident((bk, bk))],
        out_specs=q_spec,
        scratch_shapes=[
            pltpu.VMEM((2, bq, LANES), F32),
            pltpu.VMEM((2, bq, LANES), F32),
        ],
        compiler_params=pltpu.CompilerParams(
            dimension_semantics=("arbitrary", "arbitrary", "arbitrary"),
            vmem_limit_bytes=VMEM_LIMIT),
        name="sb_attn",
    )(q, k, v, tri)


def kernel(x, norm_mix, norm_ffn, norm_kv, norm_final, ssm_w_in, ssm_a_re, ssm_a_im, ssm_log_dt, ssm_b_re, ssm_b_im, ssm_c_re, ssm_c_im, ssm_d, ssm_w_glu, kv_w, attn_w_q, attn_w_o, ffn_w_up, ffn_conv_w, ffn_conv_b, ffn_w_down):
    depth = norm_mix.shape[0]
    n_a = ssm_w_in.shape[0]
    assert depth == 2 and n_a == 1 and attn_w_q.shape[0] == 1
    x = _s5_mixer(x, norm_mix[0], ssm_w_in[0], ssm_a_re[0], ssm_a_im[0], ssm_log_dt[0],
                  ssm_b_re[0], ssm_b_im[0], ssm_c_re[0], ssm_c_im[0], ssm_d[0], ssm_w_glu[0])
    x = _conv_ffn(x, norm_ffn[0], ffn_w_up[0], ffn_conv_w[0], ffn_conv_b[0], ffn_w_down[0])
    q, k, v = _qkv_proj(x, norm_mix[1], norm_kv, attn_w_q[0], kv_w)
    attn = _sb_attention(q, k, v)
    return _conv_ffn(x, norm_ffn[1], ffn_w_up[1], ffn_conv_w[1], ffn_conv_b[1], ffn_w_down[1],
                     attn=attn, w_o=attn_w_o[0], final_w=norm_final)
```

```python
import functools
import math

import jax
import jax.numpy as jnp
from jax import lax
from jax.experimental import pallas as pl
from jax.experimental.pallas import tpu as pltpu

F32 = jnp.float32
BF16 = jnp.bfloat16

D_MODEL = 1024
N_GROUPS = 64
SSM_GROUP = 16
STATE = 64
N_HEADS = 16
HEAD_DIM = 64
D_FF = 2816
EPS = 1e-6

LANES = 128
SUBLANES = 8
N_LANE_BLOCKS = D_MODEL // LANES
GROUPS_PER_PAIR = 2
N_PAIRS = N_GROUPS // GROUPS_PER_PAIR
PAIRS_PER_LANE_BLOCK = N_PAIRS // N_LANE_BLOCKS

S5_TT = 64
FFN_TL = 512
FF_CHUNK = 256
ATT_BQ = 512
ATT_BK = 256
Q_SCALE = HEAD_DIM ** -0.5 * math.log2(math.e)
MASKED_SCORE = -1e30
SOFTPLUS2_LINEAR_ABOVE = 32.0
SCAN_PAIRS = 8
VMEM_LIMIT = 56 * 1024 * 1024


def _rms(x, g):
    ms = jnp.mean(x * x, axis=-1, keepdims=True)
    return x * lax.rsqrt(ms + EPS) * g


def _resident(shape):
    nd = len(shape)
    return pl.BlockSpec(shape, lambda *_: (0,) * nd, pipeline_mode=pl.Buffered(1))


def _s5_kernel(x_ref, nw_ref, win_ref, wb_ref, wc_ref, are_ref, aim_ref, d_ref, wglu_ref,
               o_ref, utb_scr, bre_scr, bim_scr, ytb_scr, yg_scr, sre_scr, sim_scr, *, batch, tt):
    rows = batch * tt

    @pl.when(pl.program_id(0) == 0)
    def _():
        sre_scr[...] = jnp.zeros_like(sre_scr)
        sim_scr[...] = jnp.zeros_like(sim_scr)

    xb = x_ref[...].reshape(rows, D_MODEL)
    h = _rms(xb, nw_ref[...]).astype(BF16)
    u = jnp.dot(h, win_ref[...], preferred_element_type=F32)

    for c in range(N_LANE_BLOCKS):
        for b in range(batch):
            utb_scr[c, pl.ds(b, tt, stride=batch), :] = u[b * tt:(b + 1) * tt, c * LANES:(c + 1) * LANES]

    def bproj(c, _):
        bu = jnp.dot(utb_scr[c].astype(BF16), wb_ref[c], preferred_element_type=F32)
        for q in range(PAIRS_PER_LANE_BLOCK):
            k = c * PAIRS_PER_LANE_BLOCK + q
            bre_scr[k] = bu[:, 2 * q * LANES:(2 * q + 1) * LANES]
            bim_scr[k] = bu[:, (2 * q + 1) * LANES:(2 * q + 2) * LANES]
        return 0
    lax.fori_loop(0, N_LANE_BLOCKS, bproj, 0)

    for kg in range(N_PAIRS // SCAN_PAIRS):
        ks = [kg * SCAN_PAIRS + j for j in range(SCAN_PAIRS)]
        ar = [are_ref[k] for k in ks]
        ai = [aim_ref[k] for k in ks]
        init = tuple(s for k in ks for s in (sre_scr[k], sim_scr[k]))

        def step(t, carry):
            row = pl.multiple_of(t * SUBLANES, SUBLANES)
            new = []
            for j, k in enumerate(ks):
                sr, si = carry[2 * j], carry[2 * j + 1]
                nr = ar[j] * sr - ai[j] * si + bre_scr[k, pl.ds(row, SUBLANES), :]
                ni = ar[j] * si + ai[j] * sr + bim_scr[k, pl.ds(row, SUBLANES), :]
                bre_scr[k, pl.ds(row, SUBLANES), :] = nr
                bim_scr[k, pl.ds(row, SUBLANES), :] = ni
                new += [nr, ni]
            return tuple(new)

        fin = lax.fori_loop(0, tt, step, init, unroll=2)
        for j, k in enumerate(ks):
            sre_scr[k] = fin[2 * j]
            sim_scr[k] = fin[2 * j + 1]

    for c in range(N_LANE_BLOCKS):
        parts = []
        for q in range(PAIRS_PER_LANE_BLOCK):
            k = c * PAIRS_PER_LANE_BLOCK + q
            parts += [bre_scr[k].astype(BF16), bim_scr[k].astype(BF16)]
        lhs = jnp.concatenate(parts, axis=1)
        y = jnp.dot(lhs, wc_ref[c], preferred_element_type=F32)
        y = y + d_ref[:, c * LANES:(c + 1) * LANES] * utb_scr[c]
        ytb_scr[c] = jax.nn.gelu(y)
        for b in range(batch):
            yg_scr[pl.ds(b * tt, tt), c * LANES:(c + 1) * LANES] = (
                ytb_scr[c, pl.ds(b, tt, stride=batch), :].astype(BF16))

    z = jnp.dot(yg_scr[...], wglu_ref[...], preferred_element_type=F32)
    out = x_ref[...].reshape(rows, D_MODEL) + z[:, :D_MODEL] * jax.nn.sigmoid(z[:, D_MODEL:])
    o_ref[...] = out.reshape(batch, tt, D_MODEL)


def _s5_tables(a_re, a_im, log_dt, b_re, b_im, c_re, c_im, batch):
    dt = jnp.exp(log_dt)[:, None]
    mag = jnp.exp(a_re * dt)
    ab_re = mag * jnp.cos(a_im * dt)
    ab_im = mag * jnp.sin(a_im * dt)
    den = a_re * a_re + a_im * a_im
    f_re = ((ab_re - 1.0) * a_re + ab_im * a_im) / den
    f_im = (ab_im * a_re - (ab_re - 1.0) * a_im) / den
    bb_re = f_re[..., None] * b_re - f_im[..., None] * b_im
    bb_im = f_re[..., None] * b_im + f_im[..., None] * b_re
    groups_per_block = N_GROUPS // N_LANE_BLOCKS
    bb = jnp.stack([bb_re, bb_im]).reshape(2, N_PAIRS, GROUPS_PER_PAIR, STATE, SSM_GROUP)
    pair = jnp.arange(N_PAIRS)[:, None]
    member = jnp.arange(GROUPS_PER_PAIR)[None, :]
    local_group = (GROUPS_PER_PAIR * pair) % groups_per_block + member
    onehot = (local_group[..., None] == jnp.arange(groups_per_block)).astype(F32)
    wb = jnp.einsum('rkjph,kjg->kghrjp', bb, onehot).reshape(N_PAIRS, LANES, 2 * LANES)
    wb = wb.reshape(N_LANE_BLOCKS, PAIRS_PER_LANE_BLOCK, LANES, 2 * LANES).transpose(0, 2, 1, 3)
    wb = wb.reshape(N_LANE_BLOCKS, LANES, PAIRS_PER_LANE_BLOCK * 2 * LANES)
    cc = jnp.stack([c_re, -c_im]).reshape(
        2, N_LANE_BLOCKS, PAIRS_PER_LANE_BLOCK, GROUPS_PER_PAIR, SSM_GROUP, STATE)
    wc = jnp.einsum('rcqjhp,qa,jb->cqrjpabh', cc, jnp.eye(PAIRS_PER_LANE_BLOCK, dtype=F32),
                    jnp.eye(GROUPS_PER_PAIR, dtype=F32)).reshape(N_LANE_BLOCKS, D_MODEL, LANES)
    a_re_t = jnp.broadcast_to(ab_re.reshape(N_PAIRS, 1, LANES), (N_PAIRS, batch, LANES))
    a_im_t = jnp.broadcast_to(ab_im.reshape(N_PAIRS, 1, LANES), (N_PAIRS, batch, LANES))
    return wb.astype(BF16), wc.astype(BF16), a_re_t, a_im_t


def _s5_mixer(x, norm_w, w_in, a_re, a_im, log_dt, b_re, b_im, c_re, c_im, d_skip, w_glu):
    batch, seq, _ = x.shape
    assert batch == SUBLANES, "the scan keeps one timestep of all batches in one sublane tile"
    tt = S5_TT
    rows = batch * tt
    wb, wc, a_re_t, a_im_t = _s5_tables(a_re, a_im, log_dt, b_re, b_im, c_re, c_im, batch)
    kern = functools.partial(_s5_kernel, batch=batch, tt=tt)
    return pl.pallas_call(
        kern,
        out_shape=jax.ShapeDtypeStruct(x.shape, F32),
        grid=(seq // tt,),
        in_specs=[
            pl.BlockSpec((batch, tt, D_MODEL), lambda i: (0, i, 0)),
            _resident((1, D_MODEL)),
            _resident((D_MODEL, D_MODEL)),
            _resident((N_LANE_BLOCKS, LANES, PAIRS_PER_LANE_BLOCK * 2 * LANES)),
            _resident((N_LANE_BLOCKS, D_MODEL, LANES)),
            _resident((N_PAIRS, batch, LANES)),
            _resident((N_PAIRS, batch, LANES)),
            _resident((1, D_MODEL)),
            _resident((D_MODEL, 2 * D_MODEL)),
        ],
        out_specs=pl.BlockSpec((batch, tt, D_MODEL), lambda i: (0, i, 0)),
        scratch_shapes=[
            pltpu.VMEM((N_LANE_BLOCKS, rows, LANES), F32),
            pltpu.VMEM((N_PAIRS, rows, LANES), F32),
            pltpu.VMEM((N_PAIRS, rows, LANES), F32),
            pltpu.VMEM((N_LANE_BLOCKS, rows, LANES), F32),
            pltpu.VMEM((rows, D_MODEL), BF16),
            pltpu.VMEM((N_PAIRS, batch, LANES), F32),
            pltpu.VMEM((N_PAIRS, batch, LANES), F32),
        ],
        compiler_params=pltpu.CompilerParams(
            dimension_semantics=("arbitrary",), vmem_limit_bytes=VMEM_LIMIT),
        name="s5_mixer",
    )(x, norm_w.reshape(1, D_MODEL), w_in.astype(BF16), wb, wc, a_re_t, a_im_t,
      d_skip.reshape(1, D_MODEL), w_glu.astype(BF16))


def _ffn_kernel(*refs, tl, pre_proj, final_norm):
    refs = list(refs)
    x_ref = refs.pop(0)
    if pre_proj:
        a_ref, wo_ref = refs.pop(0), refs.pop(0)
    nw_ref, wup_ref, cw_ref, cb_ref, wd_ref = (refs.pop(0) for _ in range(5))
    if final_norm:
        nf_ref = refs.pop(0)
    o_ref, gs_scr, act_scr = refs

    @pl.when(pl.program_id(1) == 0)
    def _():
        gs_scr[0:SUBLANES, :] = jnp.zeros((SUBLANES, D_FF), F32)

    x = x_ref[...]
    if pre_proj:
        x = x + jnp.dot(a_ref[...], wo_ref[...], preferred_element_type=F32)
    h = _rms(x, nw_ref[...]).astype(BF16)
    for c in range(D_FF // FF_CHUNK):
        sl = slice(c * FF_CHUNK, (c + 1) * FF_CHUNK)
        g = jnp.dot(h, wup_ref[:, sl], preferred_element_type=F32)
        up = jnp.dot(h, wup_ref[:, D_FF + c * FF_CHUNK:D_FF + (c + 1) * FF_CHUNK],
                     preferred_element_type=F32)
        gs_scr[SUBLANES:SUBLANES + tl, sl] = g
        g1 = gs_scr[SUBLANES - 1:SUBLANES - 1 + tl, sl]
        g2 = gs_scr[SUBLANES - 2:SUBLANES - 2 + tl, sl]
        gc = cw_ref[0:1, sl] * g2 + cw_ref[1:2, sl] * g1 + cw_ref[2:3, sl] * g + cb_ref[:, sl]
        act_scr[:, sl] = (gc * jax.nn.sigmoid(gc) * up).astype(BF16)
        gs_scr[0:SUBLANES, sl] = g[tl - SUBLANES:tl]
    y = x + jnp.dot(act_scr[...], wd_ref[...], preferred_element_type=F32)
    if final_norm:
        y = _rms(y, nf_ref[...])
    o_ref[...] = y


def _conv_ffn(x, norm_w, w_up, conv_w, conv_b, w_down, attn=None, w_o=None, final_w=None):
    batch, seq, _ = x.shape
    tl = FFN_TL
    pre_proj = attn is not None
    final_norm = final_w is not None
    row_spec = pl.BlockSpec((None, tl, D_MODEL), lambda b, i: (b, i, 0))
    in_specs = [row_spec]
    args = [x]
    if pre_proj:
        in_specs += [row_spec, _resident((D_MODEL, D_MODEL))]
        args += [attn, w_o.astype(BF16)]
    in_specs += [_resident((1, D_MODEL)), _resident((D_MODEL, 2 * D_FF)), _resident((3, D_FF)),
                 _resident((1, D_FF)), _resident((D_FF, D_MODEL))]
    args += [norm_w.reshape(1, D_MODEL), w_up.astype(BF16), conv_w.reshape(3, D_FF),
             conv_b.reshape(1, D_FF), w_down.astype(BF16)]
    if final_norm:
        in_specs.append(_resident((1, D_MODEL)))
        args.append(final_w.reshape(1, D_MODEL))
    kern = functools.partial(_ffn_kernel, tl=tl, pre_proj=pre_proj, final_norm=final_norm)
    return pl.pallas_call(
        kern,
        out_shape=jax.ShapeDtypeStruct(x.shape, F32),
        grid=(batch, seq // tl),
        in_specs=in_specs,
        out_specs=row_spec,
        scratch_shapes=[
            pltpu.VMEM((SUBLANES + tl, D_FF), F32),
            pltpu.VMEM((tl, D_FF), BF16),
        ],
        compiler_params=pltpu.CompilerParams(
            dimension_semantics=("arbitrary", "arbitrary"), vmem_limit_bytes=VMEM_LIMIT),
        name="conv_ffn",
    )(*args)


def _qkv_kernel(x_ref, nq_ref, nkv_ref, wq_ref, wkv_ref, q_ref, k_ref, v_ref):
    x = x_ref[...]
    hq = _rms(x, nq_ref[...]).astype(BF16)
    hkv = _rms(x, nkv_ref[...]).astype(BF16)
    q = jnp.dot(hq, wq_ref[...], preferred_element_type=F32) * Q_SCALE
    kv = jnp.dot(hkv, wkv_ref[...], preferred_element_type=F32)
    q_ref[...] = q.astype(BF16)
    k_ref[...] = kv[:, :D_MODEL].astype(BF16)
    v_ref[...] = kv[:, D_MODEL:].astype(BF16)


def _qkv_proj(x, norm_q, norm_kv, w_q, w_kv):
    batch, seq, _ = x.shape
    tl = FFN_TL
    row_spec = pl.BlockSpec((None, tl, D_MODEL), lambda b, i: (b, i, 0))
    out = jax.ShapeDtypeStruct(x.shape, BF16)
    return pl.pallas_call(
        _qkv_kernel,
        out_shape=(out, out, out),
        grid=(batch, seq // tl),
        in_specs=[row_spec, _resident((1, D_MODEL)), _resident((1, D_MODEL)),
                  _resident((D_MODEL, D_MODEL)), _resident((D_MODEL, 2 * D_MODEL))],
        out_specs=(row_spec, row_spec, row_spec),
        compiler_params=pltpu.CompilerParams(
            dimension_semantics=("arbitrary", "arbitrary"), vmem_limit_bytes=VMEM_LIMIT),
        name="qkv_proj",
    )(x, norm_q.reshape(1, D_MODEL), norm_kv.reshape(1, D_MODEL), w_q.astype(BF16),
      w_kv.astype(BF16))


def _softplus2(z):
    return jnp.where(z > SOFTPLUS2_LINEAR_ABOVE, z, jnp.log2(1.0 + jnp.exp2(z)))


def _attn_kernel(q_ref, k_ref, v_ref, tri_ref, o_ref, z_scr, sp_scr, rs_scr, acc_scr, rem_scr,
                 *, bq, bk):
    qi = pl.program_id(2)
    n_blocks = (qi + 1) * (bq // bk)
    q2 = q_ref[...]
    lane = lax.broadcasted_iota(jnp.int32, (bq, LANES), 1)
    first = lane < HEAD_DIM
    zero = jnp.zeros_like(q2)
    q_heads = (jnp.where(first, q2, zero), jnp.where(first, zero, q2))

    acc_scr[...] = jnp.zeros_like(acc_scr)
    rem_scr[...] = jnp.zeros_like(rem_scr)

    def stage1(j, slot, masked, r0=0):
        start = pl.multiple_of(j * bk, bk)
        kb = k_ref[pl.ds(start, bk), :]
        if masked:
            row = lax.broadcasted_iota(jnp.int32, (bq - r0, bk), 0) + (qi * bq + r0)
            col = lax.broadcasted_iota(jnp.int32, (bq - r0, bk), 1) + j * bk
            causal = col < row
        for hd in range(2):
            z = lax.dot_general(q_heads[hd][r0:], kb, (((1,), (1,)), ((), ())),
                                preferred_element_type=F32)
            if masked:
                z = jnp.where(causal, z, MASKED_SCORE)
            sp = _softplus2(z)
            z_scr[slot, hd, r0:] = z
            sp_scr[slot, hd, r0:] = sp.astype(BF16)
            rs_scr[slot, hd, r0:] = jnp.broadcast_to(
                jnp.sum(sp, axis=-1, keepdims=True), (bq - r0, LANES))

    def stage2(j, slot, r0=0):
        start = pl.multiple_of(j * bk, bk)
        vb = v_ref[pl.ds(start, bk), :]
        for hd in range(2):
            cs = jnp.dot(sp_scr[slot, hd, r0:], tri_ref[...], preferred_element_type=F32)
            rem = rem_scr[hd, r0:]
            w = jnp.exp2(z_scr[slot, hd, r0:] - cs - jnp.concatenate([rem] * (bk // LANES), axis=1))
            acc_scr[hd, r0:] += jnp.dot(w.astype(BF16), vb, preferred_element_type=F32)
            rem_scr[hd, r0:] = rem + rs_scr[slot, hd, r0:]

    stage1(n_blocks - 1, 0, True, r0=bk)
    stage2(n_blocks - 1, 0, r0=bk)
    stage1(n_blocks - 2, 1, True)

    def body(pp, _):
        p = 2 * pp
        stage2(n_blocks - p, 1)
        stage1(n_blocks - 1 - p, 0, False)
        stage2(n_blocks - 1 - p, 0)
        stage1(n_blocks - 2 - p, 1, False)
        return 0
    lax.fori_loop(1, n_blocks // 2, body, 0)
    stage2(0, 1)

    o_ref[...] = jnp.where(first, acc_scr[0], acc_scr[1]).astype(BF16)


def _sb_attention(q, k, v):
    batch, seq, _ = q.shape
    bq, bk = ATT_BQ, ATT_BK
    assert bq == 2 * bk, "the pipeline masks exactly the first two key blocks"
    tri = (jnp.arange(bk)[:, None] >= jnp.arange(bk)[None, :]).astype(BF16)
    kern = functools.partial(_attn_kernel, bq=bq, bk=bk)
    kv_spec = pl.BlockSpec((None, seq, LANES), lambda b, hp, i: (b, 0, hp))
    q_spec = pl.BlockSpec((None, bq, LANES), lambda b, hp, i: (b, i, hp))
    return pl.pallas_call(
        kern,
        out_shape=jax.ShapeDtypeStruct(q.shape, BF16),
        grid=(batch, N_HEADS * HEAD_DIM // LANES, seq // bq),
        in_specs=[q_spec, kv_spec, kv_spec, _resident((bk, bk))],
        out_specs=q_spec,
        scratch_shapes=[
            pltpu.VMEM((2, 2, bq, bk), F32),
            pltpu.VMEM((2, 2, bq, bk), BF16),
            pltpu.VMEM((2, 2, bq, LANES), F32),
            pltpu.VMEM((2, bq, LANES), F32),
            pltpu.VMEM((2, bq, LANES), F32),
        ],
        compiler_params=pltpu.CompilerParams(
            dimension_semantics=("arbitrary", "arbitrary", "arbitrary"),
            vmem_limit_bytes=VMEM_LIMIT),
        name="sb_attn",
    )(q, k, v, tri)


def kernel(x, norm_mix, norm_ffn, norm_kv, norm_final, ssm_w_in, ssm_a_re, ssm_a_im, ssm_log_dt, ssm_b_re, ssm_b_im, ssm_c_re, ssm_c_im, ssm_d, ssm_w_glu, kv_w, attn_w_q, attn_w_o, ffn_w_up, ffn_conv_w, ffn_conv_b, ffn_w_down):
    depth = norm_mix.shape[0]
    n_a = ssm_w_in.shape[0]
    assert depth == 2 and n_a == 1 and attn_w_q.shape[0] == 1
    x = _s5_mixer(x, norm_mix[0], ssm_w_in[0], ssm_a_re[0], ssm_a_im[0], ssm_log_dt[0],
                  ssm_b_re[0], ssm_b_im[0], ssm_c_re[0], ssm_c_im[0], ssm_d[0], ssm_w_glu[0])
    x = _conv_ffn(x, norm_ffn[0], ffn_w_up[0], ffn_conv_w[0], ffn_conv_b[0], ffn_w_down[0])
    q, k, v = _qkv_proj(x, norm_mix[1], norm_kv, attn_w_q[0], kv_w)
    attn = _sb_attention(q, k, v)
    return _conv_ffn(x, norm_ffn[1], ffn_w_up[1], ffn_conv_w[1], ffn_conv_b[1], ffn_w_down[1],
                     attn=attn, w_o=attn_w_o[0], final_w=norm_final)
```

```python
import functools
import math

import jax
import jax.numpy as jnp
from jax import lax
from jax.experimental import pallas as pl
from jax.experimental.pallas import tpu as pltpu

F32 = jnp.float32
BF16 = jnp.bfloat16

D_MODEL = 1024
N_GROUPS = 64
SSM_GROUP = 16
STATE = 64
N_HEADS = 16
HEAD_DIM = 64
D_FF = 2816
EPS = 1e-6

LANES = 128
SUBLANES = 8
N_LANE_BLOCKS = D_MODEL // LANES
GROUPS_PER_PAIR = 2
N_PAIRS = N_GROUPS // GROUPS_PER_PAIR
PAIRS_PER_LANE_BLOCK = N_PAIRS // N_LANE_BLOCKS

S5_TT = 64
FFN_TL = 512
FF_CHUNK = 256
ATT_BQ = 512
ATT_BK = 256
Q_SCALE = HEAD_DIM ** -0.5 * math.log2(math.e)
MASKED_SCORE = -1e30
SOFTPLUS2_LINEAR_ABOVE = 32.0
LOG2_W_UNDERFLOW = 150.0
VMEM_LIMIT = 56 * 1024 * 1024


def _rms(x, g):
    ms = jnp.mean(x * x, axis=-1, keepdims=True)
    return x * lax.rsqrt(ms + EPS) * g


def _resident(shape):
    nd = len(shape)
    return pl.BlockSpec(shape, lambda *_: (0,) * nd, pipeline_mode=pl.Buffered(1))


def _s5_kernel(x_ref, nw_ref, win_ref, wb_ref, wc_ref, are_ref, aim_ref, d_ref, wglu_ref,
               o_ref, utb_scr, bre_scr, bim_scr, ytb_scr, yg_scr, sre_scr, sim_scr, *, batch, tt):
    rows = batch * tt

    @pl.when(pl.program_id(0) == 0)
    def _():
        sre_scr[...] = jnp.zeros_like(sre_scr)
        sim_scr[...] = jnp.zeros_like(sim_scr)

    xb = x_ref[...].reshape(rows, D_MODEL)
    h = _rms(xb, nw_ref[...]).astype(BF16)
    u = jnp.dot(h, win_ref[...], preferred_element_type=F32)

    for c in range(N_LANE_BLOCKS):
        for b in range(batch):
            utb_scr[c, pl.ds(b, tt, stride=batch), :] = u[b * tt:(b + 1) * tt, c * LANES:(c + 1) * LANES]

    def bproj(c):
        bu = jnp.dot(utb_scr[c].astype(BF16), wb_ref[c], preferred_element_type=F32)
        for q in range(PAIRS_PER_LANE_BLOCK):
            k = c * PAIRS_PER_LANE_BLOCK + q
            bre_scr[k] = bu[:, 2 * q * LANES:(2 * q + 1) * LANES]
            bim_scr[k] = bu[:, (2 * q + 1) * LANES:(2 * q + 2) * LANES]

    def scan(c):
        for k in range(c * PAIRS_PER_LANE_BLOCK, (c + 1) * PAIRS_PER_LANE_BLOCK):
            ar, ai = are_ref[k], aim_ref[k]
            sr, si = sre_scr[k], sim_scr[k]
            for t in range(tt):
                rows_t = pl.ds(t * SUBLANES, SUBLANES)
                nr = ar * sr - ai * si + bre_scr[k, rows_t, :]
                ni = ar * si + ai * sr + bim_scr[k, rows_t, :]
                bre_scr[k, rows_t, :] = nr
                bim_scr[k, rows_t, :] = ni
                sr, si = nr, ni
            sre_scr[k] = sr
            sim_scr[k] = si

    def cproj(c):
        parts = []
        for q in range(PAIRS_PER_LANE_BLOCK):
            k = c * PAIRS_PER_LANE_BLOCK + q
            parts += [bre_scr[k].astype(BF16), bim_scr[k].astype(BF16)]
        lhs = jnp.concatenate(parts, axis=1)
        y = jnp.dot(lhs, wc_ref[c], preferred_element_type=F32)
        y = y + d_ref[:, c * LANES:(c + 1) * LANES] * utb_scr[c]
        ytb_scr[c] = jax.nn.gelu(y)
        for b in range(batch):
            yg_scr[pl.ds(b * tt, tt), c * LANES:(c + 1) * LANES] = (
                ytb_scr[c, pl.ds(b, tt, stride=batch), :].astype(BF16))

    for c in range(N_LANE_BLOCKS + 2):
        if c < N_LANE_BLOCKS:
            bproj(c)
        if 1 <= c <= N_LANE_BLOCKS:
            scan(c - 1)
        if c >= 2:
            cproj(c - 2)

    z = jnp.dot(yg_scr[...], wglu_ref[...], preferred_element_type=F32)
    out = x_ref[...].reshape(rows, D_MODEL) + z[:, :D_MODEL] * jax.nn.sigmoid(z[:, D_MODEL:])
    o_ref[...] = out.reshape(batch, tt, D_MODEL)


def _s5_tables(a_re, a_im, log_dt, b_re, b_im, c_re, c_im, batch):
    dt = jnp.exp(log_dt)[:, None]
    mag = jnp.exp(a_re * dt)
    ab_re = mag * jnp.cos(a_im * dt)
    ab_im = mag * jnp.sin(a_im * dt)
    den = a_re * a_re + a_im * a_im
    f_re = ((ab_re - 1.0) * a_re + ab_im * a_im) / den
    f_im = (ab_im * a_re - (ab_re - 1.0) * a_im) / den
    bb_re = f_re[..., None] * b_re - f_im[..., None] * b_im
    bb_im = f_re[..., None] * b_im + f_im[..., None] * b_re
    groups_per_block = N_GROUPS // N_LANE_BLOCKS
    bb = jnp.stack([bb_re, bb_im]).reshape(2, N_PAIRS, GROUPS_PER_PAIR, STATE, SSM_GROUP)
    pair = jnp.arange(N_PAIRS)[:, None]
    member = jnp.arange(GROUPS_PER_PAIR)[None, :]
    local_group = (GROUPS_PER_PAIR * pair) % groups_per_block + member
    onehot = (local_group[..., None] == jnp.arange(groups_per_block)).astype(F32)
    wb = jnp.einsum('rkjph,kjg->kghrjp', bb, onehot).reshape(N_PAIRS, LANES, 2 * LANES)
    wb = wb.reshape(N_LANE_BLOCKS, PAIRS_PER_LANE_BLOCK, LANES, 2 * LANES).transpose(0, 2, 1, 3)
    wb = wb.reshape(N_LANE_BLOCKS, LANES, PAIRS_PER_LANE_BLOCK * 2 * LANES)
    cc = jnp.stack([c_re, -c_im]).reshape(
        2, N_LANE_BLOCKS, PAIRS_PER_LANE_BLOCK, GROUPS_PER_PAIR, SSM_GROUP, STATE)
    wc = jnp.einsum('rcqjhp,qa,jb->cqrjpabh', cc, jnp.eye(PAIRS_PER_LANE_BLOCK, dtype=F32),
                    jnp.eye(GROUPS_PER_PAIR, dtype=F32)).reshape(N_LANE_BLOCKS, D_MODEL, LANES)
    a_re_t = jnp.broadcast_to(ab_re.reshape(N_PAIRS, 1, LANES), (N_PAIRS, batch, LANES))
    a_im_t = jnp.broadcast_to(ab_im.reshape(N_PAIRS, 1, LANES), (N_PAIRS, batch, LANES))
    return wb.astype(BF16), wc.astype(BF16), a_re_t, a_im_t


def _s5_mixer(x, norm_w, w_in, a_re, a_im, log_dt, b_re, b_im, c_re, c_im, d_skip, w_glu):
    batch, seq, _ = x.shape
    assert batch == SUBLANES, "the scan keeps one timestep of all batches in one sublane tile"
    tt = S5_TT
    rows = batch * tt
    wb, wc, a_re_t, a_im_t = _s5_tables(a_re, a_im, log_dt, b_re, b_im, c_re, c_im, batch)
    kern = functools.partial(_s5_kernel, batch=batch, tt=tt)
    return pl.pallas_call(
        kern,
        out_shape=jax.ShapeDtypeStruct(x.shape, F32),
        grid=(seq // tt,),
        in_specs=[
            pl.BlockSpec((batch, tt, D_MODEL), lambda i: (0, i, 0)),
            _resident((1, D_MODEL)),
            _resident((D_MODEL, D_MODEL)),
            _resident((N_LANE_BLOCKS, LANES, PAIRS_PER_LANE_BLOCK * 2 * LANES)),
            _resident((N_LANE_BLOCKS, D_MODEL, LANES)),
            _resident((N_PAIRS, batch, LANES)),
            _resident((N_PAIRS, batch, LANES)),
            _resident((1, D_MODEL)),
            _resident((D_MODEL, 2 * D_MODEL)),
        ],
        out_specs=pl.BlockSpec((batch, tt, D_MODEL), lambda i: (0, i, 0)),
        scratch_shapes=[
            pltpu.VMEM((N_LANE_BLOCKS, rows, LANES), F32),
            pltpu.VMEM((N_PAIRS, rows, LANES), F32),
            pltpu.VMEM((N_PAIRS, rows, LANES), F32),
            pltpu.VMEM((N_LANE_BLOCKS, rows, LANES), F32),
            pltpu.VMEM((rows, D_MODEL), BF16),
            pltpu.VMEM((N_PAIRS, batch, LANES), F32),
            pltpu.VMEM((N_PAIRS, batch, LANES), F32),
        ],
        compiler_params=pltpu.CompilerParams(
            dimension_semantics=("arbitrary",), vmem_limit_bytes=VMEM_LIMIT),
        name="s5_mixer",
    )(x, norm_w.reshape(1, D_MODEL), w_in.astype(BF16), wb, wc, a_re_t, a_im_t,
      d_skip.reshape(1, D_MODEL), w_glu.astype(BF16))


def _ffn_kernel(*refs, tl, pre_proj, final_norm):
    refs = list(refs)
    x_ref = refs.pop(0)
    if pre_proj:
        a_ref, wo_ref = refs.pop(0), refs.pop(0)
    nw_ref, wup_ref, cw_ref, cb_ref, wd_ref = (refs.pop(0) for _ in range(5))
    if final_norm:
        nf_ref = refs.pop(0)
    o_ref, gs_scr, act_scr = refs

    @pl.when(pl.program_id(1) == 0)
    def _():
        gs_scr[0:SUBLANES, :] = jnp.zeros((SUBLANES, D_FF), F32)

    x = x_ref[...]
    if pre_proj:
        x = x + jnp.dot(a_ref[...], wo_ref[...], preferred_element_type=F32)
    h = _rms(x, nw_ref[...]).astype(BF16)
    for c in range(D_FF // FF_CHUNK):
        sl = slice(c * FF_CHUNK, (c + 1) * FF_CHUNK)
        g = jnp.dot(h, wup_ref[:, sl], preferred_element_type=F32)
        up = jnp.dot(h, wup_ref[:, D_FF + c * FF_CHUNK:D_FF + (c + 1) * FF_CHUNK],
                     preferred_element_type=F32)
        gs_scr[SUBLANES:SUBLANES + tl, sl] = g
        g1 = gs_scr[SUBLANES - 1:SUBLANES - 1 + tl, sl]
        g2 = gs_scr[SUBLANES - 2:SUBLANES - 2 + tl, sl]
        gc = cw_ref[0:1, sl] * g2 + cw_ref[1:2, sl] * g1 + cw_ref[2:3, sl] * g + cb_ref[:, sl]
        act_scr[:, sl] = (gc * jax.nn.sigmoid(gc) * up).astype(BF16)
        gs_scr[0:SUBLANES, sl] = g[tl - SUBLANES:tl]
    y = x + jnp.dot(act_scr[...], wd_ref[...], preferred_element_type=F32)
    if final_norm:
        y = _rms(y, nf_ref[...])
    o_ref[...] = y


def _conv_ffn(x, norm_w, w_up, conv_w, conv_b, w_down, attn=None, w_o=None, final_w=None):
    batch, seq, _ = x.shape
    tl = FFN_TL
    pre_proj = attn is not None
    final_norm = final_w is not None
    row_spec = pl.BlockSpec((None, tl, D_MODEL), lambda b, i: (b, i, 0))
    in_specs = [row_spec]
    args = [x]
    if pre_proj:
        in_specs += [row_spec, _resident((D_MODEL, D_MODEL))]
        args += [attn, w_o.astype(BF16)]
    in_specs += [_resident((1, D_MODEL)), _resident((D_MODEL, 2 * D_FF)), _resident((3, D_FF)),
                 _resident((1, D_FF)), _resident((D_FF, D_MODEL))]
    args += [norm_w.reshape(1, D_MODEL), w_up.astype(BF16), conv_w.reshape(3, D_FF),
             conv_b.reshape(1, D_FF), w_down.astype(BF16)]
    if final_norm:
        in_specs.append(_resident((1, D_MODEL)))
        args.append(final_w.reshape(1, D_MODEL))
    kern = functools.partial(_ffn_kernel, tl=tl, pre_proj=pre_proj, final_norm=final_norm)
    return pl.pallas_call(
        kern,
        out_shape=jax.ShapeDtypeStruct(x.shape, F32),
        grid=(batch, seq // tl),
        in_specs=in_specs,
        out_specs=row_spec,
        scratch_shapes=[
            pltpu.VMEM((SUBLANES + tl, D_FF), F32),
            pltpu.VMEM((tl, D_FF), BF16),
        ],
        compiler_params=pltpu.CompilerParams(
            dimension_semantics=("arbitrary", "arbitrary"), vmem_limit_bytes=VMEM_LIMIT),
        name="conv_ffn",
    )(*args)


def _qkv_kernel(x_ref, nq_ref, nkv_ref, wq_ref, wkv_ref, q_ref, k_ref, v_ref):
    x = x_ref[...]
    hq = _rms(x, nq_ref[...]).astype(BF16)
    hkv = _rms(x, nkv_ref[...]).astype(BF16)
    q = jnp.dot(hq, wq_ref[...], preferred_element_type=F32) * Q_SCALE
    kv = jnp.dot(hkv, wkv_ref[...], preferred_element_type=F32)
    q_ref[...] = q.astype(BF16)
    k_ref[...] = kv[:, :D_MODEL].astype(BF16)
    v_ref[...] = kv[:, D_MODEL:].astype(BF16)


def _qkv_proj(x, norm_q, norm_kv, w_q, w_kv):
    batch, seq, _ = x.shape
    tl = FFN_TL
    row_spec = pl.BlockSpec((None, tl, D_MODEL), lambda b, i: (b, i, 0))
    out = jax.ShapeDtypeStruct(x.shape, BF16)
    return pl.pallas_call(
        _qkv_kernel,
        out_shape=(out, out, out),
        grid=(batch, seq // tl),
        in_specs=[row_spec, _resident((1, D_MODEL)), _resident((1, D_MODEL)),
                  _resident((D_MODEL, D_MODEL)), _resident((D_MODEL, 2 * D_MODEL))],
        out_specs=(row_spec, row_spec, row_spec),
        compiler_params=pltpu.CompilerParams(
            dimension_semantics=("arbitrary", "arbitrary"), vmem_limit_bytes=VMEM_LIMIT),
        name="qkv_proj",
    )(x, norm_q.reshape(1, D_MODEL), norm_kv.reshape(1, D_MODEL), w_q.astype(BF16),
      w_kv.astype(BF16))


def _softplus2(z):
    return jnp.where(z > SOFTPLUS2_LINEAR_ABOVE, z, jnp.log2(1.0 + jnp.exp2(z)))


def _attn_kernel(q_ref, k_ref, v_ref, tri_ref, o_ref, z_scr, sp_scr, rs_scr, acc_scr, rem_scr,
                 *, bq, bk):
    qi = pl.program_id(2)
    n_blocks = (qi + 1) * (bq // bk)
    q2 = q_ref[...]
    lane = lax.broadcasted_iota(jnp.int32, (bq, LANES), 1)
    first = lane < HEAD_DIM
    zero = jnp.zeros_like(q2)
    q_heads = (jnp.where(first, q2, zero), jnp.where(first, zero, q2))

    acc_scr[...] = jnp.zeros_like(acc_scr)
    rem_scr[...] = jnp.zeros_like(rem_scr)

    def stage1(j, slot, diagonal=False, r0=0, valid=None):
        start = pl.multiple_of(j * bk, bk)
        kb = k_ref[pl.ds(start, bk), :]
        if diagonal:
            row = lax.broadcasted_iota(jnp.int32, (bq - r0, bk), 0) + r0
            col = lax.broadcasted_iota(jnp.int32, (bq - r0, bk), 1) + (bq - (slot + 1) * bk)
            keep = col < row
        elif valid is not None:
            keep = valid
        else:
            keep = None
        for hd in range(2):
            z = lax.dot_general(q_heads[hd][r0:], kb, (((1,), (1,)), ((), ())),
                                preferred_element_type=F32)
            if keep is not None:
                z = jnp.where(keep, z, MASKED_SCORE)
            sp = _softplus2(z)
            z_scr[slot, hd, r0:] = z
            sp_scr[slot, hd, r0:] = sp.astype(BF16)
            rs_scr[slot, hd, r0:] = jnp.broadcast_to(
                jnp.sum(sp, axis=-1, keepdims=True), (bq - r0, LANES))

    def stage2(j, slot, r0=0):
        start = pl.multiple_of(j * bk, bk)
        vb = v_ref[pl.ds(start, bk), :]
        for hd in range(2):
            cs = jnp.dot(sp_scr[slot, hd, r0:], tri_ref[...], preferred_element_type=F32)
            rem = rem_scr[hd, r0:]
            logw = jnp.minimum(z_scr[slot, hd, r0:] - cs, 0.0)
            w = jnp.exp2(logw - jnp.concatenate([rem] * (bk // LANES), axis=1))
            acc_scr[hd, r0:] += jnp.dot(w.astype(BF16), vb, preferred_element_type=F32)
            rem_scr[hd, r0:] = rem + rs_scr[slot, hd, r0:]

    def blocks_beyond_are_dead(slot=None):
        r = rem_scr[...]
        if slot is not None:
            r = r + rs_scr[slot]
        return jnp.min(r) >= LOG2_W_UNDERFLOW

    has_p2 = n_blocks > 2
    j2 = jnp.maximum(n_blocks - 3, 0)
    stage1(n_blocks - 1, 0, diagonal=True, r0=bk)
    stage1(n_blocks - 2, 1, diagonal=True)
    stage1(j2, 2, valid=has_p2)
    stage2(n_blocks - 1, 0, r0=bk)
    stage2(n_blocks - 2, 1)
    stage2(j2, 2)

    @pl.when(jnp.logical_and(n_blocks > 3, jnp.logical_not(blocks_beyond_are_dead())))
    def _():
        stage1(n_blocks - 4, 1)

        def cond(carry):
            pp, dead = carry
            return jnp.logical_and(pp < qi - 1, jnp.logical_not(dead))

        def body(carry):
            pp, _ = carry
            p = 4 + 2 * pp
            stage2(n_blocks - p, 1)
            stage1(n_blocks - 1 - p, 0)
            stage2(n_blocks - 1 - p, 0)
            stage1(n_blocks - 2 - p, 1)
            return pp + 1, blocks_beyond_are_dead(slot=1)

        trips, _ = lax.while_loop(cond, body, (jnp.int32(0), blocks_beyond_are_dead(slot=1)))
        stage2(n_blocks - 4 - 2 * trips, 1)

    o_ref[...] = jnp.where(first, acc_scr[0], acc_scr[1]).astype(BF16)


def _sb_attention(q, k, v):
    batch, seq, _ = q.shape
    bq, bk = ATT_BQ, ATT_BK
    assert bq == 2 * bk, "exactly the first two key blocks touch the diagonal"
    tri = (jnp.arange(bk)[:, None] >= jnp.arange(bk)[None, :]).astype(BF16)
    kern = functools.partial(_attn_kernel, bq=bq, bk=bk)
    kv_spec = pl.BlockSpec((None, seq, LANES), lambda b, hp, i: (b, 0, hp))
    q_spec = pl.BlockSpec((None, bq, LANES), lambda b, hp, i: (b, i, hp))
    return pl.pallas_call(
        kern,
        out_shape=jax.ShapeDtypeStruct(q.shape, BF16),
        grid=(batch, N_HEADS * HEAD_DIM // LANES, seq // bq),
        in_specs=[q_spec, kv_spec, kv_spec, _resident((bk, bk))],
        out_specs=q_spec,
        scratch_shapes=[
            pltpu.VMEM((3, 2, bq, bk), F32),
            pltpu.VMEM((3, 2, bq, bk), BF16),
            pltpu.VMEM((3, 2, bq, LANES), F32),
            pltpu.VMEM((2, bq, LANES), F32),
            pltpu.VMEM((2, bq, LANES), F32),
        ],
        compiler_params=pltpu.CompilerParams(
            dimension_semantics=("arbitrary", "arbitrary", "arbitrary"),
            vmem_limit_bytes=VMEM_LIMIT),
        name="sb_attn",
    )(q, k, v, tri)


def kernel(x, norm_mix, norm_ffn, norm_kv, norm_final, ssm_w_in, ssm_a_re, ssm_a_im, ssm_log_dt, ssm_b_re, ssm_b_im, ssm_c_re, ssm_c_im, ssm_d, ssm_w_glu, kv_w, attn_w_q, attn_w_o, ffn_w_up, ffn_conv_w, ffn_conv_b, ffn_w_down):
    depth = norm_mix.shape[0]
    n_a = ssm_w_in.shape[0]
    assert depth == 2 and n_a == 1 and attn_w_q.shape[0] == 1
    x = _s5_mixer(x, norm_mix[0], ssm_w_in[0], ssm_a_re[0], ssm_a_im[0], ssm_log_dt[0],
                  ssm_b_re[0], ssm_b_im[0], ssm_c_re[0], ssm_c_im[0], ssm_d[0], ssm_w_glu[0])
    x = _conv_ffn(x, norm_ffn[0], ffn_w_up[0], ffn_conv_w[0], ffn_conv_b[0], ffn_w_down[0])
    q, k, v = _qkv_proj(x, norm_mix[1], norm_kv, attn_w_q[0], kv_w)
    attn = _sb_attention(q, k, v)
    return _conv_ffn(x, norm_ffn[1], ffn_w_up[1], ffn_conv_w[1], ffn_conv_b[1], ffn_w_down[1],
                     attn=attn, w_o=attn_w_o[0], final_w=norm_final)
```

```python
import functools
import math

import jax
import jax.numpy as jnp
from jax import lax
from jax.experimental import pallas as pl
from jax.experimental.pallas import tpu as pltpu

F32 = jnp.float32
BF16 = jnp.bfloat16

D_MODEL = 1024
N_GROUPS = 64
SSM_GROUP = 16
STATE = 64
N_HEADS = 16
HEAD_DIM = 64
D_FF = 2816
EPS = 1e-6

LANES = 128
SUBLANES = 8
N_LANE_BLOCKS = D_MODEL // LANES
GROUPS_PER_PAIR = 2
N_PAIRS = N_GROUPS // GROUPS_PER_PAIR
PAIRS_PER_LANE_BLOCK = N_PAIRS // N_LANE_BLOCKS

S5_TT = 64
FFN_TL = 512
FF_CHUNK = 256
ATT_BQ = 256
ATT_BK = 256
ATT_Q_BLOCKS_PER_STEP = 4
Q_SCALE = HEAD_DIM ** -0.5 * math.log2(math.e)
MASKED_SCORE = -1e30
SOFTPLUS2_LINEAR_ABOVE = 32.0
LOG2_W_UNDERFLOW = 150.0
VMEM_LIMIT = 56 * 1024 * 1024


def _rms(x, g):
    ms = jnp.mean(x * x, axis=-1, keepdims=True)
    return x * lax.rsqrt(ms + EPS) * g


def _resident(shape):
    nd = len(shape)
    return pl.BlockSpec(shape, lambda *_: (0,) * nd, pipeline_mode=pl.Buffered(1))


def _s5_kernel(x_ref, nw_ref, win_ref, wb_ref, wc_ref, are_ref, aim_ref, d_ref, wglu_ref,
               o_ref, utb_scr, bre_scr, bim_scr, ytb_scr, yg_scr, sre_scr, sim_scr, *, batch, tt):
    rows = batch * tt

    @pl.when(pl.program_id(0) == 0)
    def _():
        sre_scr[...] = jnp.zeros_like(sre_scr)
        sim_scr[...] = jnp.zeros_like(sim_scr)

    xb = x_ref[...].reshape(rows, D_MODEL)
    h = _rms(xb, nw_ref[...]).astype(BF16)
    u = jnp.dot(h, win_ref[...], preferred_element_type=F32)

    for c in range(N_LANE_BLOCKS):
        for b in range(batch):
            utb_scr[c, pl.ds(b, tt, stride=batch), :] = u[b * tt:(b + 1) * tt, c * LANES:(c + 1) * LANES]

    def bproj(c):
        bu = jnp.dot(utb_scr[c].astype(BF16), wb_ref[c], preferred_element_type=F32)
        for q in range(PAIRS_PER_LANE_BLOCK):
            k = c * PAIRS_PER_LANE_BLOCK + q
            bre_scr[k] = bu[:, 2 * q * LANES:(2 * q + 1) * LANES]
            bim_scr[k] = bu[:, (2 * q + 1) * LANES:(2 * q + 2) * LANES]

    def scan(c):
        for k in range(c * PAIRS_PER_LANE_BLOCK, (c + 1) * PAIRS_PER_LANE_BLOCK):
            ar, ai = are_ref[k], aim_ref[k]
            sr, si = sre_scr[k], sim_scr[k]
            for t in range(tt):
                rows_t = pl.ds(t * SUBLANES, SUBLANES)
                nr = ar * sr - ai * si + bre_scr[k, rows_t, :]
                ni = ar * si + ai * sr + bim_scr[k, rows_t, :]
                bre_scr[k, rows_t, :] = nr
                bim_scr[k, rows_t, :] = ni
                sr, si = nr, ni
            sre_scr[k] = sr
            sim_scr[k] = si

    def cproj(c):
        parts = []
        for q in range(PAIRS_PER_LANE_BLOCK):
            k = c * PAIRS_PER_LANE_BLOCK + q
            parts += [bre_scr[k].astype(BF16), bim_scr[k].astype(BF16)]
        lhs = jnp.concatenate(parts, axis=1)
        y = jnp.dot(lhs, wc_ref[c], preferred_element_type=F32)
        y = y + d_ref[:, c * LANES:(c + 1) * LANES] * utb_scr[c]
        ytb_scr[c] = jax.nn.gelu(y)
        for b in range(batch):
            yg_scr[pl.ds(b * tt, tt), c * LANES:(c + 1) * LANES] = (
                ytb_scr[c, pl.ds(b, tt, stride=batch), :].astype(BF16))

    for c in range(N_LANE_BLOCKS + 2):
        if c < N_LANE_BLOCKS:
            bproj(c)
        if 1 <= c <= N_LANE_BLOCKS:
            scan(c - 1)
        if c >= 2:
            cproj(c - 2)

    z = jnp.dot(yg_scr[...], wglu_ref[...], preferred_element_type=F32)
    out = x_ref[...].reshape(rows, D_MODEL) + z[:, :D_MODEL] * jax.nn.sigmoid(z[:, D_MODEL:])
    o_ref[...] = out.reshape(batch, tt, D_MODEL)


def _s5_tables(a_re, a_im, log_dt, b_re, b_im, c_re, c_im, batch):
    dt = jnp.exp(log_dt)[:, None]
    mag = jnp.exp(a_re * dt)
    ab_re = mag * jnp.cos(a_im * dt)
    ab_im = mag * jnp.sin(a_im * dt)
    den = a_re * a_re + a_im * a_im
    f_re = ((ab_re - 1.0) * a_re + ab_im * a_im) / den
    f_im = (ab_im * a_re - (ab_re - 1.0) * a_im) / den
    bb_re = f_re[..., None] * b_re - f_im[..., None] * b_im
    bb_im = f_re[..., None] * b_im + f_im[..., None] * b_re
    groups_per_block = N_GROUPS // N_LANE_BLOCKS
    bb = jnp.stack([bb_re, bb_im]).reshape(2, N_PAIRS, GROUPS_PER_PAIR, STATE, SSM_GROUP)
    pair = jnp.arange(N_PAIRS)[:, None]
    member = jnp.arange(GROUPS_PER_PAIR)[None, :]
    local_group = (GROUPS_PER_PAIR * pair) % groups_per_block + member
    onehot = (local_group[..., None] == jnp.arange(groups_per_block)).astype(F32)
    wb = jnp.einsum('rkjph,kjg->kghrjp', bb, onehot).reshape(N_PAIRS, LANES, 2 * LANES)
    wb = wb.reshape(N_LANE_BLOCKS, PAIRS_PER_LANE_BLOCK, LANES, 2 * LANES).transpose(0, 2, 1, 3)
    wb = wb.reshape(N_LANE_BLOCKS, LANES, PAIRS_PER_LANE_BLOCK * 2 * LANES)
    cc = jnp.stack([c_re, -c_im]).reshape(
        2, N_LANE_BLOCKS, PAIRS_PER_LANE_BLOCK, GROUPS_PER_PAIR, SSM_GROUP, STATE)
    wc = jnp.einsum('rcqjhp,qa,jb->cqrjpabh', cc, jnp.eye(PAIRS_PER_LANE_BLOCK, dtype=F32),
                    jnp.eye(GROUPS_PER_PAIR, dtype=F32)).reshape(N_LANE_BLOCKS, D_MODEL, LANES)
    a_re_t = jnp.broadcast_to(ab_re.reshape(N_PAIRS, 1, LANES), (N_PAIRS, batch, LANES))
    a_im_t = jnp.broadcast_to(ab_im.reshape(N_PAIRS, 1, LANES), (N_PAIRS, batch, LANES))
    return wb.astype(BF16), wc.astype(BF16), a_re_t, a_im_t


def _s5_mixer(x, norm_w, w_in, a_re, a_im, log_dt, b_re, b_im, c_re, c_im, d_skip, w_glu):
    batch, seq, _ = x.shape
    assert batch == SUBLANES, "the scan keeps one timestep of all batches in one sublane tile"
    tt = S5_TT
    rows = batch * tt
    wb, wc, a_re_t, a_im_t = _s5_tables(a_re, a_im, log_dt, b_re, b_im, c_re, c_im, batch)
    kern = functools.partial(_s5_kernel, batch=batch, tt=tt)
    return pl.pallas_call(
        kern,
        out_shape=jax.ShapeDtypeStruct(x.shape, F32),
        grid=(seq // tt,),
        in_specs=[
            pl.BlockSpec((batch, tt, D_MODEL), lambda i: (0, i, 0)),
            _resident((1, D_MODEL)),
            _resident((D_MODEL, D_MODEL)),
            _resident((N_LANE_BLOCKS, LANES, PAIRS_PER_LANE_BLOCK * 2 * LANES)),
            _resident((N_LANE_BLOCKS, D_MODEL, LANES)),
            _resident((N_PAIRS, batch, LANES)),
            _resident((N_PAIRS, batch, LANES)),
            _resident((1, D_MODEL)),
            _resident((D_MODEL, 2 * D_MODEL)),
        ],
        out_specs=pl.BlockSpec((batch, tt, D_MODEL), lambda i: (0, i, 0)),
        scratch_shapes=[
            pltpu.VMEM((N_LANE_BLOCKS, rows, LANES), F32),
            pltpu.VMEM((N_PAIRS, rows, LANES), F32),
            pltpu.VMEM((N_PAIRS, rows, LANES), F32),
            pltpu.VMEM((N_LANE_BLOCKS, rows, LANES), F32),
            pltpu.VMEM((rows, D_MODEL), BF16),
            pltpu.VMEM((N_PAIRS, batch, LANES), F32),
            pltpu.VMEM((N_PAIRS, batch, LANES), F32),
        ],
        compiler_params=pltpu.CompilerParams(
            dimension_semantics=("arbitrary",), vmem_limit_bytes=VMEM_LIMIT),
        name="s5_mixer",
    )(x, norm_w.reshape(1, D_MODEL), w_in.astype(BF16), wb, wc, a_re_t, a_im_t,
      d_skip.reshape(1, D_MODEL), w_glu.astype(BF16))


def _ffn_kernel(*refs, tl, pre_proj, final_norm):
    refs = list(refs)
    x_ref = refs.pop(0)
    if pre_proj:
        a_ref, wo_ref = refs.pop(0), refs.pop(0)
    nw_ref, wup_ref, cw_ref, cb_ref, wd_ref = (refs.pop(0) for _ in range(5))
    if final_norm:
        nf_ref = refs.pop(0)
    o_ref, gs_scr, act_scr = refs

    @pl.when(pl.program_id(1) == 0)
    def _():
        gs_scr[0:SUBLANES, :] = jnp.zeros((SUBLANES, D_FF), F32)

    x = x_ref[...]
    if pre_proj:
        x = x + jnp.dot(a_ref[...], wo_ref[...], preferred_element_type=F32)
    h = _rms(x, nw_ref[...]).astype(BF16)
    gu = jnp.dot(h, wup_ref[...], preferred_element_type=F32)
    for c in range(D_FF // FF_CHUNK):
        sl = slice(c * FF_CHUNK, (c + 1) * FF_CHUNK)
        g = gu[:, sl]
        up = gu[:, D_FF + c * FF_CHUNK:D_FF + (c + 1) * FF_CHUNK]
        gs_scr[SUBLANES:SUBLANES + tl, sl] = g
        g1 = gs_scr[SUBLANES - 1:SUBLANES - 1 + tl, sl]
        g2 = gs_scr[SUBLANES - 2:SUBLANES - 2 + tl, sl]
        gc = cw_ref[0:1, sl] * g2 + cw_ref[1:2, sl] * g1 + cw_ref[2:3, sl] * g + cb_ref[:, sl]
        act_scr[:, sl] = (gc * jax.nn.sigmoid(gc) * up).astype(BF16)
        gs_scr[0:SUBLANES, sl] = g[tl - SUBLANES:tl]
    y = x + jnp.dot(act_scr[...], wd_ref[...], preferred_element_type=F32)
    if final_norm:
        y = _rms(y, nf_ref[...])
    o_ref[...] = y


def _conv_ffn(x, norm_w, w_up, conv_w, conv_b, w_down, attn=None, w_o=None, final_w=None):
    batch, seq, _ = x.shape
    tl = FFN_TL
    pre_proj = attn is not None
    final_norm = final_w is not None
    row_spec = pl.BlockSpec((None, tl, D_MODEL), lambda b, i: (b, i, 0))
    in_specs = [row_spec]
    args = [x]
    if pre_proj:
        in_specs += [row_spec, _resident((D_MODEL, D_MODEL))]
        args += [attn, w_o.astype(BF16)]
    in_specs += [_resident((1, D_MODEL)), _resident((D_MODEL, 2 * D_FF)), _resident((3, D_FF)),
                 _resident((1, D_FF)), _resident((D_FF, D_MODEL))]
    args += [norm_w.reshape(1, D_MODEL), w_up.astype(BF16), conv_w.reshape(3, D_FF),
             conv_b.reshape(1, D_FF), w_down.astype(BF16)]
    if final_norm:
        in_specs.append(_resident((1, D_MODEL)))
        args.append(final_w.reshape(1, D_MODEL))
    kern = functools.partial(_ffn_kernel, tl=tl, pre_proj=pre_proj, final_norm=final_norm)
    return pl.pallas_call(
        kern,
        out_shape=jax.ShapeDtypeStruct(x.shape, F32),
        grid=(batch, seq // tl),
        in_specs=in_specs,
        out_specs=row_spec,
        scratch_shapes=[
            pltpu.VMEM((SUBLANES + tl, D_FF), F32),
            pltpu.VMEM((tl, D_FF), BF16),
        ],
        compiler_params=pltpu.CompilerParams(
            dimension_semantics=("arbitrary", "arbitrary"), vmem_limit_bytes=VMEM_LIMIT),
        name="conv_ffn",
    )(*args)


def _qkv_kernel(x_ref, nq_ref, nkv_ref, wq_ref, wkv_ref, q_ref, k_ref, v_ref):
    x = x_ref[...]
    hq = _rms(x, nq_ref[...]).astype(BF16)
    hkv = _rms(x, nkv_ref[...]).astype(BF16)
    q = jnp.dot(hq, wq_ref[...], preferred_element_type=F32) * Q_SCALE
    kv = jnp.dot(hkv, wkv_ref[...], preferred_element_type=F32)
    q_ref[...] = q.astype(BF16)
    k_ref[...] = kv[:, :D_MODEL].astype(BF16)
    v_ref[...] = kv[:, D_MODEL:].astype(BF16)


def _qkv_proj(x, norm_q, norm_kv, w_q, w_kv):
    batch, seq, _ = x.shape
    tl = FFN_TL
    row_spec = pl.BlockSpec((None, tl, D_MODEL), lambda b, i: (b, i, 0))
    out = jax.ShapeDtypeStruct(x.shape, BF16)
    return pl.pallas_call(
        _qkv_kernel,
        out_shape=(out, out, out),
        grid=(batch, seq // tl),
        in_specs=[row_spec, _resident((1, D_MODEL)), _resident((1, D_MODEL)),
                  _resident((D_MODEL, D_MODEL)), _resident((D_MODEL, 2 * D_MODEL))],
        out_specs=(row_spec, row_spec, row_spec),
        compiler_params=pltpu.CompilerParams(
            dimension_semantics=("arbitrary", "arbitrary"), vmem_limit_bytes=VMEM_LIMIT),
        name="qkv_proj",
    )(x, norm_q.reshape(1, D_MODEL), norm_kv.reshape(1, D_MODEL), w_q.astype(BF16),
      w_kv.astype(BF16))


def _softplus2(z):
    return jnp.where(z > SOFTPLUS2_LINEAR_ABOVE, z, jnp.log2(1.0 + jnp.exp2(z)))


def _attn_kernel(q_ref, k_ref, v_ref, tri_ref, o_ref, z_scr, sp_scr, rs_scr, acc_scr, rem_scr,
                 *, bq, bk, n_sub):
    lane = lax.broadcasted_iota(jnp.int32, (bq, LANES), 1)
    first = lane < HEAD_DIM

    acc_scr[...] = jnp.zeros_like(acc_scr)
    rem_scr[...] = jnp.zeros_like(rem_scr)

    def walker(sub):
        qb = pl.program_id(2) * n_sub + sub
        n_blocks = qb + 1
        q2 = q_ref[sub * bq:(sub + 1) * bq, :]
        zero = jnp.zeros_like(q2)
        q_heads = (jnp.where(first, q2, zero), jnp.where(first, zero, q2))
        z_sub, sp_sub, rs_sub = z_scr.at[sub], sp_scr.at[sub], rs_scr.at[sub]
        acc_sub, rem_sub = acc_scr.at[sub], rem_scr.at[sub]

        def stage1(j, slot, diagonal=False, valid=None):
            start = pl.multiple_of(j * bk, bk)
            kb = k_ref[pl.ds(start, bk), :]
            if diagonal:
                row = lax.broadcasted_iota(jnp.int32, (bq, bk), 0)
                col = lax.broadcasted_iota(jnp.int32, (bq, bk), 1)
                keep = col < row
            elif valid is not None:
                keep = valid
            else:
                keep = None
            for hd in range(2):
                z = lax.dot_general(q_heads[hd], kb, (((1,), (1,)), ((), ())),
                                    preferred_element_type=F32)
                if keep is not None:
                    z = jnp.where(keep, z, MASKED_SCORE)
                sp = _softplus2(z)
                z_sub[slot, hd] = z
                sp_sub[slot, hd] = sp.astype(BF16)
                rs_sub[slot, hd] = jnp.broadcast_to(
                    jnp.sum(sp, axis=-1, keepdims=True), (bq, LANES))

        def stage2(j, slot):
            start = pl.multiple_of(j * bk, bk)
            vb = v_ref[pl.ds(start, bk), :]
            for hd in range(2):
                cs = jnp.dot(sp_sub[slot, hd], tri_ref[...], preferred_element_type=F32)
                rem = rem_sub[hd]
                logw = jnp.minimum(z_sub[slot, hd] - cs, 0.0)
                w = jnp.exp2(logw - jnp.concatenate([rem] * (bk // LANES), axis=1))
                acc_sub[hd] += jnp.dot(w.astype(BF16), vb, preferred_element_type=F32)
                rem_sub[hd] = rem + rs_sub[slot, hd]

        def blocks_beyond_are_dead(slot=None):
            r = rem_sub[...]
            if slot is not None:
                r = r + rs_sub[slot]
            return jnp.min(r) >= LOG2_W_UNDERFLOW

        def first_part():
            j1 = jnp.maximum(n_blocks - 2, 0)
            stage1(n_blocks - 1, 0, diagonal=True)
            stage2(n_blocks - 1, 0)
            stage1(j1, 1, valid=n_blocks > 1)
            stage2(j1, 1)
            return jnp.logical_and(n_blocks > 2, jnp.logical_not(blocks_beyond_are_dead()))

        def rest(more_blocks_live):
            @pl.when(more_blocks_live)
            def _():
                stage1(n_blocks - 3, 0)

                def cond(carry):
                    p, dead = carry
                    return jnp.logical_and(p < n_blocks - 1, jnp.logical_not(dead))

                def body(carry):
                    p, _ = carry
                    slot = p % 2
                    stage2(n_blocks - 1 - p, slot)
                    stage1(n_blocks - 2 - p, 1 - slot)
                    return p + 1, blocks_beyond_are_dead(slot=1 - slot)

                p_last, _ = lax.while_loop(cond, body,
                                           (jnp.int32(2), blocks_beyond_are_dead(slot=0)))
                stage2(n_blocks - 1 - p_last, p_last % 2)

        return first_part, rest

    walkers = [walker(sub) for sub in range(n_sub)]
    live = [first_part() for first_part, _ in walkers]
    for (_, rest), more_blocks_live in zip(walkers, live):
        rest(more_blocks_live)
    for sub in range(n_sub):
        o_ref[sub * bq:(sub + 1) * bq, :] = jnp.where(
            first, acc_scr[sub, 0], acc_scr[sub, 1]).astype(BF16)


def _sb_attention(q, k, v):
    batch, seq, _ = q.shape
    bq, bk, n_sub = ATT_BQ, ATT_BK, ATT_Q_BLOCKS_PER_STEP
    assert bq == bk, "exactly the first key block touches the diagonal"
    tri = (jnp.arange(bk)[:, None] >= jnp.arange(bk)[None, :]).astype(BF16)
    kern = functools.partial(_attn_kernel, bq=bq, bk=bk, n_sub=n_sub)
    kv_spec = pl.BlockSpec((None, seq, LANES), lambda b, hp, i: (b, 0, hp))
    q_spec = pl.BlockSpec((None, n_sub * bq, LANES), lambda b, hp, i: (b, i, hp))
    return pl.pallas_call(
        kern,
        out_shape=jax.ShapeDtypeStruct(q.shape, BF16),
        grid=(batch, N_HEADS * HEAD_DIM // LANES, seq // (n_sub * bq)),
        in_specs=[q_spec, kv_spec, kv_spec, _resident((bk, bk))],
        out_specs=q_spec,
        scratch_shapes=[
            pltpu.VMEM((n_sub, 2, 2, bq, bk), F32),
            pltpu.VMEM((n_sub, 2, 2, bq, bk), BF16),
            pltpu.VMEM((n_sub, 2, 2, bq, LANES), F32),
            pltpu.VMEM((n_sub, 2, bq, LANES), F32),
            pltpu.VMEM((n_sub, 2, bq, LANES), F32),
        ],
        compiler_params=pltpu.CompilerParams(
            dimension_semantics=("arbitrary", "arbitrary", "arbitrary"),
            vmem_limit_bytes=VMEM_LIMIT),
        name="sb_attn",
    )(q, k, v, tri)


def kernel(x, norm_mix, norm_ffn, norm_kv, norm_final, ssm_w_in, ssm_a_re, ssm_a_im, ssm_log_dt, ssm_b_re, ssm_b_im, ssm_c_re, ssm_c_im, ssm_d, ssm_w_glu, kv_w, attn_w_q, attn_w_o, ffn_w_up, ffn_conv_w, ffn_conv_b, ffn_w_down):
    depth = norm_mix.shape[0]
    n_a = ssm_w_in.shape[0]
    assert depth == 2 and n_a == 1 and attn_w_q.shape[0] == 1
    x = _s5_mixer(x, norm_mix[0], ssm_w_in[0], ssm_a_re[0], ssm_a_im[0], ssm_log_dt[0],
                  ssm_b_re[0], ssm_b_im[0], ssm_c_re[0], ssm_c_im[0], ssm_d[0], ssm_w_glu[0])
    x = _conv_ffn(x, norm_ffn[0], ffn_w_up[0], ffn_conv_w[0], ffn_conv_b[0], ffn_w_down[0])
    q, k, v = _qkv_proj(x, norm_mix[1], norm_kv, attn_w_q[0], kv_w)
    attn = _sb_attention(q, k, v)
    return _conv_ffn(x, norm_ffn[1], ffn_w_up[1], ffn_conv_w[1], ffn_conv_b[1], ffn_w_down[1],
                     attn=attn, w_o=attn_w_o[0], final_w=norm_final)
```

```python
import functools
import math

import jax
import jax.numpy as jnp
from jax import lax
from jax.experimental import pallas as pl
from jax.experimental.pallas import tpu as pltpu

F32 = jnp.float32
BF16 = jnp.bfloat16

D_MODEL = 1024
N_GROUPS = 64
SSM_GROUP = 16
STATE = 64
N_HEADS = 16
HEAD_DIM = 64
D_FF = 2816
EPS = 1e-6

LANES = 128
SUBLANES = 8
N_LANE_BLOCKS = D_MODEL // LANES
GROUPS_PER_PAIR = 2
N_PAIRS = N_GROUPS // GROUPS_PER_PAIR
PAIRS_PER_LANE_BLOCK = N_PAIRS // N_LANE_BLOCKS

S5_TT = 64
GLU_CHUNK = 256
FFN_TL = 512
FF_CHUNK = 256
ATT_BQ = 256
ATT_BK = 256
ATT_Q_BLOCKS_PER_STEP = 4
Q_SCALE = HEAD_DIM ** -0.5 * math.log2(math.e)
MASKED_SCORE = -1e30
SOFTPLUS2_LINEAR_ABOVE = 32.0
LOG2_W_UNDERFLOW = 150.0
VMEM_LIMIT = 56 * 1024 * 1024


def _rms(x, g):
    ms = jnp.mean(x * x, axis=-1, keepdims=True)
    return x * lax.rsqrt(ms + EPS) * g


def _resident(shape):
    nd = len(shape)
    return pl.BlockSpec(shape, lambda *_: (0,) * nd, pipeline_mode=pl.Buffered(1))


def _s5_kernel(x_ref, nw_ref, win_ref, wb_ref, wc_ref, are_ref, aim_ref, d_ref, wglu_ref,
               o_ref, utb_scr, bre_scr, bim_scr, ytb_scr, yg_scr, sre_scr, sim_scr, *, batch, tt):
    rows = batch * tt

    @pl.when(pl.program_id(0) == 0)
    def _():
        sre_scr[...] = jnp.zeros_like(sre_scr)
        sim_scr[...] = jnp.zeros_like(sim_scr)

    xb = x_ref[...].reshape(rows, D_MODEL)
    h = _rms(xb, nw_ref[...]).astype(BF16)
    u = jnp.dot(h, win_ref[...], preferred_element_type=F32)

    for c in range(N_LANE_BLOCKS):
        for b in range(batch):
            utb_scr[c, pl.ds(b, tt, stride=batch), :] = u[b * tt:(b + 1) * tt, c * LANES:(c + 1) * LANES]

    def bproj(c):
        bu = jnp.dot(utb_scr[c].astype(BF16), wb_ref[c], preferred_element_type=F32)
        for q in range(PAIRS_PER_LANE_BLOCK):
            k = c * PAIRS_PER_LANE_BLOCK + q
            bre_scr[k] = bu[:, 2 * q * LANES:(2 * q + 1) * LANES]
            bim_scr[k] = bu[:, (2 * q + 1) * LANES:(2 * q + 2) * LANES]

    def scan(c):
        for k in range(c * PAIRS_PER_LANE_BLOCK, (c + 1) * PAIRS_PER_LANE_BLOCK):
            ar, ai = are_ref[k], aim_ref[k]
            sr, si = sre_scr[k], sim_scr[k]
            for t in range(tt):
                rows_t = pl.ds(t * SUBLANES, SUBLANES)
                nr = ar * sr - ai * si + bre_scr[k, rows_t, :]
                ni = ar * si + ai * sr + bim_scr[k, rows_t, :]
                bre_scr[k, rows_t, :] = nr
                bim_scr[k, rows_t, :] = ni
                sr, si = nr, ni
            sre_scr[k] = sr
            sim_scr[k] = si

    def cproj(c):
        parts = []
        for q in range(PAIRS_PER_LANE_BLOCK):
            k = c * PAIRS_PER_LANE_BLOCK + q
            parts += [bre_scr[k].astype(BF16), bim_scr[k].astype(BF16)]
        lhs = jnp.concatenate(parts, axis=1)
        y = jnp.dot(lhs, wc_ref[c], preferred_element_type=F32)
        y = y + d_ref[:, c * LANES:(c + 1) * LANES] * utb_scr[c]
        ytb_scr[c] = jax.nn.gelu(y)
        for b in range(batch):
            yg_scr[pl.ds(b * tt, tt), c * LANES:(c + 1) * LANES] = (
                ytb_scr[c, pl.ds(b, tt, stride=batch), :].astype(BF16))

    for c in range(N_LANE_BLOCKS + 2):
        if c < N_LANE_BLOCKS:
            bproj(c)
        if 1 <= c <= N_LANE_BLOCKS:
            scan(c - 1)
        if c >= 2:
            cproj(c - 2)

    yg = yg_scr[...]
    for c in range(D_MODEL // GLU_CHUNK):
        sl = slice(c * GLU_CHUNK, (c + 1) * GLU_CHUNK)
        val = jnp.dot(yg, wglu_ref[:, sl], preferred_element_type=F32)
        gate = jnp.dot(yg, wglu_ref[:, D_MODEL + c * GLU_CHUNK:D_MODEL + (c + 1) * GLU_CHUNK],
                       preferred_element_type=F32)
        out = x_ref[:, :, sl].reshape(rows, GLU_CHUNK) + val * jax.nn.sigmoid(gate)
        o_ref[:, :, sl] = out.reshape(batch, tt, GLU_CHUNK)


def _s5_tables(a_re, a_im, log_dt, b_re, b_im, c_re, c_im, batch):
    dt = jnp.exp(log_dt)[:, None]
    mag = jnp.exp(a_re * dt)
    ab_re = mag * jnp.cos(a_im * dt)
    ab_im = mag * jnp.sin(a_im * dt)
    den = a_re * a_re + a_im * a_im
    f_re = ((ab_re - 1.0) * a_re + ab_im * a_im) / den
    f_im = (ab_im * a_re - (ab_re - 1.0) * a_im) / den
    bb_re = f_re[..., None] * b_re - f_im[..., None] * b_im
    bb_im = f_re[..., None] * b_im + f_im[..., None] * b_re
    groups_per_block = N_GROUPS // N_LANE_BLOCKS
    bb = jnp.stack([bb_re, bb_im]).reshape(2, N_PAIRS, GROUPS_PER_PAIR, STATE, SSM_GROUP)
    pair = jnp.arange(N_PAIRS)[:, None]
    member = jnp.arange(GROUPS_PER_PAIR)[None, :]
    local_group = (GROUPS_PER_PAIR * pair) % groups_per_block + member
    onehot = (local_group[..., None] == jnp.arange(groups_per_block)).astype(F32)
    wb = jnp.einsum('rkjph,kjg->kghrjp', bb, onehot).reshape(N_PAIRS, LANES, 2 * LANES)
    wb = wb.reshape(N_LANE_BLOCKS, PAIRS_PER_LANE_BLOCK, LANES, 2 * LANES).transpose(0, 2, 1, 3)
    wb = wb.reshape(N_LANE_BLOCKS, LANES, PAIRS_PER_LANE_BLOCK * 2 * LANES)
    cc = jnp.stack([c_re, -c_im]).reshape(
        2, N_LANE_BLOCKS, PAIRS_PER_LANE_BLOCK, GROUPS_PER_PAIR, SSM_GROUP, STATE)
    wc = jnp.einsum('rcqjhp,qa,jb->cqrjpabh', cc, jnp.eye(PAIRS_PER_LANE_BLOCK, dtype=F32),
                    jnp.eye(GROUPS_PER_PAIR, dtype=F32)).reshape(N_LANE_BLOCKS, D_MODEL, LANES)
    a_re_t = jnp.broadcast_to(ab_re.reshape(N_PAIRS, 1, LANES), (N_PAIRS, batch, LANES))
    a_im_t = jnp.broadcast_to(ab_im.reshape(N_PAIRS, 1, LANES), (N_PAIRS, batch, LANES))
    return wb.astype(BF16), wc.astype(BF16), a_re_t, a_im_t


def _s5_mixer(x, norm_w, w_in, a_re, a_im, log_dt, b_re, b_im, c_re, c_im, d_skip, w_glu):
    batch, seq, _ = x.shape
    assert batch == SUBLANES, "the scan keeps one timestep of all batches in one sublane tile"
    tt = S5_TT
    rows = batch * tt
    wb, wc, a_re_t, a_im_t = _s5_tables(a_re, a_im, log_dt, b_re, b_im, c_re, c_im, batch)
    kern = functools.partial(_s5_kernel, batch=batch, tt=tt)
    return pl.pallas_call(
        kern,
        out_shape=jax.ShapeDtypeStruct(x.shape, F32),
        grid=(seq // tt,),
        in_specs=[
            pl.BlockSpec((batch, tt, D_MODEL), lambda i: (0, i, 0)),
            _resident((1, D_MODEL)),
            _resident((D_MODEL, D_MODEL)),
            _resident((N_LANE_BLOCKS, LANES, PAIRS_PER_LANE_BLOCK * 2 * LANES)),
            _resident((N_LANE_BLOCKS, D_MODEL, LANES)),
            _resident((N_PAIRS, batch, LANES)),
            _resident((N_PAIRS, batch, LANES)),
            _resident((1, D_MODEL)),
            _resident((D_MODEL, 2 * D_MODEL)),
        ],
        out_specs=pl.BlockSpec((batch, tt, D_MODEL), lambda i: (0, i, 0)),
        scratch_shapes=[
            pltpu.VMEM((N_LANE_BLOCKS, rows, LANES), F32),
            pltpu.VMEM((N_PAIRS, rows, LANES), F32),
            pltpu.VMEM((N_PAIRS, rows, LANES), F32),
            pltpu.VMEM((N_LANE_BLOCKS, rows, LANES), F32),
            pltpu.VMEM((rows, D_MODEL), BF16),
            pltpu.VMEM((N_PAIRS, batch, LANES), F32),
            pltpu.VMEM((N_PAIRS, batch, LANES), F32),
        ],
        compiler_params=pltpu.CompilerParams(
            dimension_semantics=("arbitrary",), vmem_limit_bytes=VMEM_LIMIT),
        name="s5_mixer",
    )(x, norm_w.reshape(1, D_MODEL), w_in.astype(BF16), wb, wc, a_re_t, a_im_t,
      d_skip.reshape(1, D_MODEL), w_glu.astype(BF16))


def _ffn_kernel(*refs, tl, pre_proj, final_norm):
    refs = list(refs)
    x_ref = refs.pop(0)
    if pre_proj:
        a_ref, wo_ref = refs.pop(0), refs.pop(0)
    nw_ref, wup_ref, cw_ref, cb_ref, wd_ref = (refs.pop(0) for _ in range(5))
    if final_norm:
        nf_ref = refs.pop(0)
    o_ref, gs_scr, act_scr = refs

    @pl.when(pl.program_id(1) == 0)
    def _():
        gs_scr[0:SUBLANES, :] = jnp.zeros((SUBLANES, D_FF), F32)

    x = x_ref[...]
    if pre_proj:
        x = x + jnp.dot(a_ref[...], wo_ref[...], preferred_element_type=F32)
    h = _rms(x, nw_ref[...]).astype(BF16)
    gu = jnp.dot(h, wup_ref[...], preferred_element_type=F32)
    for c in range(D_FF // FF_CHUNK):
        sl = slice(c * FF_CHUNK, (c + 1) * FF_CHUNK)
        g = gu[:, sl]
        up = gu[:, D_FF + c * FF_CHUNK:D_FF + (c + 1) * FF_CHUNK]
        gs_scr[SUBLANES:SUBLANES + tl, sl] = g
        g1 = gs_scr[SUBLANES - 1:SUBLANES - 1 + tl, sl]
        g2 = gs_scr[SUBLANES - 2:SUBLANES - 2 + tl, sl]
        gc = cw_ref[0:1, sl] * g2 + cw_ref[1:2, sl] * g1 + cw_ref[2:3, sl] * g + cb_ref[:, sl]
        act_scr[:, sl] = (gc * jax.nn.sigmoid(gc) * up).astype(BF16)
        gs_scr[0:SUBLANES, sl] = g[tl - SUBLANES:tl]
    y = x + jnp.dot(act_scr[...], wd_ref[...], preferred_element_type=F32)
    if final_norm:
        y = _rms(y, nf_ref[...])
    o_ref[...] = y


def _conv_ffn(x, norm_w, w_up, conv_w, conv_b, w_down, attn=None, w_o=None, final_w=None):
    batch, seq, _ = x.shape
    tl = FFN_TL
    pre_proj = attn is not None
    final_norm = final_w is not None
    row_spec = pl.BlockSpec((None, tl, D_MODEL), lambda b, i: (b, i, 0))
    in_specs = [row_spec]
    args = [x]
    if pre_proj:
        in_specs += [row_spec, _resident((D_MODEL, D_MODEL))]
        args += [attn, w_o.astype(BF16)]
    in_specs += [_resident((1, D_MODEL)), _resident((D_MODEL, 2 * D_FF)), _resident((3, D_FF)),
                 _resident((1, D_FF)), _resident((D_FF, D_MODEL))]
    args += [norm_w.reshape(1, D_MODEL), w_up.astype(BF16), conv_w.reshape(3, D_FF),
             conv_b.reshape(1, D_FF), w_down.astype(BF16)]
    if final_norm:
        in_specs.append(_resident((1, D_MODEL)))
        args.append(final_w.reshape(1, D_MODEL))
    kern = functools.partial(_ffn_kernel, tl=tl, pre_proj=pre_proj, final_norm=final_norm)
    return pl.pallas_call(
        kern,
        out_shape=jax.ShapeDtypeStruct(x.shape, F32),
        grid=(batch, seq // tl),
        in_specs=in_specs,
        out_specs=row_spec,
        scratch_shapes=[
            pltpu.VMEM((SUBLANES + tl, D_FF), F32),
            pltpu.VMEM((tl, D_FF), BF16),
        ],
        compiler_params=pltpu.CompilerParams(
            dimension_semantics=("arbitrary", "arbitrary"), vmem_limit_bytes=VMEM_LIMIT),
        name="conv_ffn",
    )(*args)


def _qkv_kernel(x_ref, nq_ref, nkv_ref, wq_ref, wkv_ref, q_ref, k_ref, v_ref):
    x = x_ref[...]
    hq = _rms(x, nq_ref[...]).astype(BF16)
    hkv = _rms(x, nkv_ref[...]).astype(BF16)
    q = jnp.dot(hq, wq_ref[...], preferred_element_type=F32) * Q_SCALE
    kv = jnp.dot(hkv, wkv_ref[...], preferred_element_type=F32)
    q_ref[...] = q.astype(BF16)
    k_ref[...] = kv[:, :D_MODEL].astype(BF16)
    v_ref[...] = kv[:, D_MODEL:].astype(BF16)


def _qkv_proj(x, norm_q, norm_kv, w_q, w_kv):
    batch, seq, _ = x.shape
    tl = FFN_TL
    row_spec = pl.BlockSpec((None, tl, D_MODEL), lambda b, i: (b, i, 0))
    out = jax.ShapeDtypeStruct(x.shape, BF16)
    return pl.pallas_call(
        _qkv_kernel,
        out_shape=(out, out, out),
        grid=(batch, seq // tl),
        in_specs=[row_spec, _resident((1, D_MODEL)), _resident((1, D_MODEL)),
                  _resident((D_MODEL, D_MODEL)), _resident((D_MODEL, 2 * D_MODEL))],
        out_specs=(row_spec, row_spec, row_spec),
        compiler_params=pltpu.CompilerParams(
            dimension_semantics=("arbitrary", "arbitrary"), vmem_limit_bytes=VMEM_LIMIT),
        name="qkv_proj",
    )(x, norm_q.reshape(1, D_MODEL), norm_kv.reshape(1, D_MODEL), w_q.astype(BF16),
      w_kv.astype(BF16))


def _softplus2(z):
    return jnp.where(z > SOFTPLUS2_LINEAR_ABOVE, z, jnp.log2(1.0 + jnp.exp2(z)))


def _attn_kernel(q_ref, k_ref, v_ref, tri_ref, o_ref, z_scr, sp_scr, rs_scr, acc_scr, rem_scr,
                 *, bq, bk, n_sub):
    lane = lax.broadcasted_iota(jnp.int32, (bq, LANES), 1)
    first = lane < HEAD_DIM

    acc_scr[...] = jnp.zeros_like(acc_scr)
    rem_scr[...] = jnp.zeros_like(rem_scr)

    def walker(sub):
        qb = pl.program_id(2) * n_sub + sub
        n_blocks = qb + 1
        q2 = q_ref[sub * bq:(sub + 1) * bq, :]
        zero = jnp.zeros_like(q2)
        q_rows = jnp.concatenate([jnp.where(first, q2, zero), jnp.where(first, zero, q2)], axis=0)
        z_sub, sp_sub, rs_sub = z_scr.at[sub], sp_scr.at[sub], rs_scr.at[sub]
        acc_sub, rem_sub = acc_scr.at[sub], rem_scr.at[sub]

        def stage1(j, slot, diagonal=False, valid=None):
            start = pl.multiple_of(j * bk, bk)
            kb = k_ref[pl.ds(start, bk), :]
            if diagonal:
                row = lax.broadcasted_iota(jnp.int32, (bq, bk), 0)
                col = lax.broadcasted_iota(jnp.int32, (bq, bk), 1)
                keep = col < row
            elif valid is not None:
                keep = valid
            else:
                keep = None
            if diagonal:
                keep = jnp.concatenate([keep, keep], axis=0)
            z = lax.dot_general(q_rows, kb, (((1,), (1,)), ((), ())), preferred_element_type=F32)
            if keep is not None:
                z = jnp.where(keep, z, MASKED_SCORE)
            sp = _softplus2(z)
            z_sub[slot] = z.reshape(2, bq, bk)
            sp_sub[slot] = sp.astype(BF16).reshape(2, bq, bk)
            rs_sub[slot] = jnp.broadcast_to(
                jnp.sum(sp, axis=-1, keepdims=True), (2 * bq, LANES)).reshape(2, bq, LANES)

        def stage2(j, slot):
            start = pl.multiple_of(j * bk, bk)
            vb = v_ref[pl.ds(start, bk), :]
            cs = jnp.dot(sp_sub[slot].reshape(2 * bq, bk), tri_ref[...],
                         preferred_element_type=F32)
            rem = rem_sub[...].reshape(2 * bq, LANES)
            logw = jnp.minimum(z_sub[slot].reshape(2 * bq, bk) - cs, 0.0)
            w = jnp.exp2(logw - jnp.concatenate([rem] * (bk // LANES), axis=1))
            pv = jnp.dot(w.astype(BF16), vb, preferred_element_type=F32)
            acc_sub[...] += pv.reshape(2, bq, LANES)
            rem_sub[...] = (rem + rs_sub[slot].reshape(2 * bq, LANES)).reshape(2, bq, LANES)

        def blocks_beyond_are_dead(slot=None):
            r = rem_sub[...]
            if slot is not None:
                r = r + rs_sub[slot]
            return jnp.min(r) >= LOG2_W_UNDERFLOW

        def first_part():
            j1 = jnp.maximum(n_blocks - 2, 0)
            stage1(n_blocks - 1, 0, diagonal=True)
            stage2(n_blocks - 1, 0)
            stage1(j1, 1, valid=n_blocks > 1)
            stage2(j1, 1)
            return jnp.logical_and(n_blocks > 2, jnp.logical_not(blocks_beyond_are_dead()))

        def rest(more_blocks_live):
            @pl.when(more_blocks_live)
            def _():
                stage1(n_blocks - 3, 0)

                def cond(carry):
                    p, dead = carry
                    return jnp.logical_and(p < n_blocks - 1, jnp.logical_not(dead))

                def body(carry):
                    p, _ = carry
                    slot = p % 2
                    stage2(n_blocks - 1 - p, slot)
                    stage1(n_blocks - 2 - p, 1 - slot)
                    return p + 1, blocks_beyond_are_dead(slot=1 - slot)

                p_last, _ = lax.while_loop(cond, body,
                                           (jnp.int32(2), blocks_beyond_are_dead(slot=0)))
                stage2(n_blocks - 1 - p_last, p_last % 2)

        return first_part, rest

    walkers = [walker(sub) for sub in range(n_sub)]
    live = [first_part() for first_part, _ in walkers]
    for (_, rest), more_blocks_live in zip(walkers, live):
        rest(more_blocks_live)
    for sub in range(n_sub):
        o_ref[sub * bq:(sub + 1) * bq, :] = jnp.where(
            first, acc_scr[sub, 0], acc_scr[sub, 1]).astype(BF16)


def _sb_attention(q, k, v):
    batch, seq, _ = q.shape
    bq, bk, n_sub = ATT_BQ, ATT_BK, ATT_Q_BLOCKS_PER_STEP
    assert bq == bk, "exactly the first key block touches the diagonal"
    tri = (jnp.arange(bk)[:, None] >= jnp.arange(bk)[None, :]).astype(BF16)
    kern = functools.partial(_attn_kernel, bq=bq, bk=bk, n_sub=n_sub)
    kv_spec = pl.BlockSpec((None, seq, LANES), lambda b, hp, i: (b, 0, hp))
    q_spec = pl.BlockSpec((None, n_sub * bq, LANES), lambda b, hp, i: (b, i, hp))
    return pl.pallas_call(
        kern,
        out_shape=jax.ShapeDtypeStruct(q.shape, BF16),
        grid=(batch, N_HEADS * HEAD_DIM // LANES, seq // (n_sub * bq)),
        in_specs=[q_spec, kv_spec, kv_spec, _resident((bk, bk))],
        out_specs=q_spec,
        scratch_shapes=[
            pltpu.VMEM((n_sub, 2, 2, bq, bk), F32),
            pltpu.VMEM((n_sub, 2, 2, bq, bk), BF16),
            pltpu.VMEM((n_sub, 2, 2, bq, LANES), F32),
            pltpu.VMEM((n_sub, 2, bq, LANES), F32),
            pltpu.VMEM((n_sub, 2, bq, LANES), F32),
        ],
        compiler_params=pltpu.CompilerParams(
            dimension_semantics=("arbitrary", "arbitrary", "arbitrary"),
            vmem_limit_bytes=VMEM_LIMIT),
        name="sb_attn",
    )(q, k, v, tri)


def kernel(x, norm_mix, norm_ffn, norm_kv, norm_final, ssm_w_in, ssm_a_re, ssm_a_im, ssm_log_dt, ssm_b_re, ssm_b_im, ssm_c_re, ssm_c_im, ssm_d, ssm_w_glu, kv_w, attn_w_q, attn_w_o, ffn_w_up, ffn_conv_w, ffn_conv_b, ffn_w_down):
    depth = norm_mix.shape[0]
    n_a = ssm_w_in.shape[0]
    assert depth == 2 and n_a == 1 and attn_w_q.shape[0] == 1
    x = _s5_mixer(x, norm_mix[0], ssm_w_in[0], ssm_a_re[0], ssm_a_im[0], ssm_log_dt[0],
                  ssm_b_re[0], ssm_b_im[0], ssm_c_re[0], ssm_c_im[0], ssm_d[0], ssm_w_glu[0])
    x = _conv_ffn(x, norm_ffn[0], ffn_w_up[0], ffn_conv_w[0], ffn_conv_b[0], ffn_w_down[0])
    q, k, v = _qkv_proj(x, norm_mix[1], norm_kv, attn_w_q[0], kv_w)
    attn = _sb_attention(q, k, v)
    return _conv_ffn(x, norm_ffn[1], ffn_w_up[1], ffn_conv_w[1], ffn_conv_b[1], ffn_w_down[1],
                     attn=attn, w_o=attn_w_o[0], final_w=norm_final)
```

```python
import functools
import math

import jax
import jax.numpy as jnp
from jax import lax
from jax.experimental import pallas as pl
from jax.experimental.pallas import tpu as pltpu

F32 = jnp.float32
BF16 = jnp.bfloat16

D_MODEL = 1024
N_GROUPS = 64
SSM_GROUP = 16
STATE = 64
N_HEADS = 16
HEAD_DIM = 64
D_FF = 2816
EPS = 1e-6

LANES = 128
SUBLANES = 8
N_LANE_BLOCKS = D_MODEL // LANES
GROUPS_PER_PAIR = 2
N_PAIRS = N_GROUPS // GROUPS_PER_PAIR
PAIRS_PER_LANE_BLOCK = N_PAIRS // N_LANE_BLOCKS

S5_TT = 64
GLU_CHUNK = 256
FFN_TL = 512
FF_CHUNK = 256
ATT_BQ = 256
ATT_BK = 256
ATT_Q_BLOCKS_PER_STEP = 4
Q_SCALE = HEAD_DIM ** -0.5 * math.log2(math.e)
MASKED_SCORE = -1e30
SOFTPLUS2_LINEAR_ABOVE = 32.0
LOG2_W_UNDERFLOW = 150.0
VMEM_LIMIT = 56 * 1024 * 1024


def _rms(x, g):
    ms = jnp.mean(x * x, axis=-1, keepdims=True)
    return x * lax.rsqrt(ms + EPS) * g


def _resident(shape):
    nd = len(shape)
    return pl.BlockSpec(shape, lambda *_: (0,) * nd, pipeline_mode=pl.Buffered(1))


BF16_SUBLANES = 16
CAST_BLOCK_ROWS = 128


def _cast_riders(weights, n_steps, step_of):
    in_specs, out_specs, out_shapes, args = [], [], [], []
    for stacked, layer in weights:
        _, n_rows, n_cols = stacked.shape
        rows = n_rows // n_steps
        if n_rows % n_steps or rows % BF16_SUBLANES:
            rows = CAST_BLOCK_ROWS
        n_blocks = n_rows // rows
        assert n_rows % rows == 0 and n_blocks <= n_steps

        def block(*grid_idx, n_blocks=n_blocks):
            return jnp.minimum(step_of(*grid_idx), n_blocks - 1)

        in_specs.append(pl.BlockSpec(
            (None, rows, n_cols), lambda *g, layer=layer, block=block: (layer, block(*g), 0)))
        out_specs.append(pl.BlockSpec((rows, n_cols), lambda *g, block=block: (block(*g), 0)))
        out_shapes.append(jax.ShapeDtypeStruct((n_rows, n_cols), BF16))
        args.append(stacked)
    return in_specs, out_specs, out_shapes, args


def _run_cast_riders(src_refs, dst_refs):
    for src, dst in zip(src_refs, dst_refs):
        dst[...] = src[...].astype(BF16)


def _s5_kernel(*refs, batch, tt, n_riders):
    x_ref, nw_ref, win_ref, wb_ref, wc_ref, are_ref, aim_ref, d_ref, wglu_ref = refs[:9]
    refs = refs[9:]
    rider_src, o_ref, rider_dst = refs[:n_riders], refs[n_riders], refs[n_riders + 1:2 * n_riders + 1]
    utb_scr, bre_scr, bim_scr, ytb_scr, yg_scr, sre_scr, sim_scr = refs[2 * n_riders + 1:]
    rows = batch * tt
    _run_cast_riders(rider_src, rider_dst)

    @pl.when(pl.program_id(0) == 0)
    def _():
        sre_scr[...] = jnp.zeros_like(sre_scr)
        sim_scr[...] = jnp.zeros_like(sim_scr)

    xb = x_ref[...].reshape(rows, D_MODEL)
    h = _rms(xb, nw_ref[...]).astype(BF16)
    u = jnp.dot(h, win_ref[...], preferred_element_type=F32)

    for c in range(N_LANE_BLOCKS):
        for b in range(batch):
            utb_scr[c, pl.ds(b, tt, stride=batch), :] = u[b * tt:(b + 1) * tt, c * LANES:(c + 1) * LANES]

    def bproj(c):
        bu = jnp.dot(utb_scr[c].astype(BF16), wb_ref[c], preferred_element_type=F32)
        for q in range(PAIRS_PER_LANE_BLOCK):
            k = c * PAIRS_PER_LANE_BLOCK + q
            bre_scr[k] = bu[:, 2 * q * LANES:(2 * q + 1) * LANES]
            bim_scr[k] = bu[:, (2 * q + 1) * LANES:(2 * q + 2) * LANES]

    def scan(c):
        for k in range(c * PAIRS_PER_LANE_BLOCK, (c + 1) * PAIRS_PER_LANE_BLOCK):
            ar, ai = are_ref[k], aim_ref[k]
            sr, si = sre_scr[k], sim_scr[k]
            for t in range(tt):
                rows_t = pl.ds(t * SUBLANES, SUBLANES)
                nr = ar * sr - ai * si + bre_scr[k, rows_t, :]
                ni = ar * si + ai * sr + bim_scr[k, rows_t, :]
                bre_scr[k, rows_t, :] = nr
                bim_scr[k, rows_t, :] = ni
                sr, si = nr, ni
            sre_scr[k] = sr
            sim_scr[k] = si

    def cproj(c):
        parts = []
        for q in range(PAIRS_PER_LANE_BLOCK):
            k = c * PAIRS_PER_LANE_BLOCK + q
            parts += [bre_scr[k].astype(BF16), bim_scr[k].astype(BF16)]
        lhs = jnp.concatenate(parts, axis=1)
        y = jnp.dot(lhs, wc_ref[c], preferred_element_type=F32)
        y = y + d_ref[:, c * LANES:(c + 1) * LANES] * utb_scr[c]
        ytb_scr[c] = jax.nn.gelu(y)
        for b in range(batch):
            yg_scr[pl.ds(b * tt, tt), c * LANES:(c + 1) * LANES] = (
                ytb_scr[c, pl.ds(b, tt, stride=batch), :].astype(BF16))

    for c in range(N_LANE_BLOCKS + 2):
        if c < N_LANE_BLOCKS:
            bproj(c)
        if 1 <= c <= N_LANE_BLOCKS:
            scan(c - 1)
        if c >= 2:
            cproj(c - 2)

    yg = yg_scr[...]
    for c in range(D_MODEL // GLU_CHUNK):
        sl = slice(c * GLU_CHUNK, (c + 1) * GLU_CHUNK)
        val = jnp.dot(yg, wglu_ref[:, sl], preferred_element_type=F32)
        gate = jnp.dot(yg, wglu_ref[:, D_MODEL + c * GLU_CHUNK:D_MODEL + (c + 1) * GLU_CHUNK],
                       preferred_element_type=F32)
        out = x_ref[:, :, sl].reshape(rows, GLU_CHUNK) + val * jax.nn.sigmoid(gate)
        o_ref[:, :, sl] = out.reshape(batch, tt, GLU_CHUNK)


def _s5_tables(a_re, a_im, log_dt, b_re, b_im, c_re, c_im, batch):
    dt = jnp.exp(log_dt)[:, None]
    mag = jnp.exp(a_re * dt)
    ab_re = mag * jnp.cos(a_im * dt)
    ab_im = mag * jnp.sin(a_im * dt)
    den = a_re * a_re + a_im * a_im
    f_re = ((ab_re - 1.0) * a_re + ab_im * a_im) / den
    f_im = (ab_im * a_re - (ab_re - 1.0) * a_im) / den
    bb_re = f_re[..., None] * b_re - f_im[..., None] * b_im
    bb_im = f_re[..., None] * b_im + f_im[..., None] * b_re
    groups_per_block = N_GROUPS // N_LANE_BLOCKS
    bb = jnp.stack([bb_re, bb_im]).reshape(2, N_PAIRS, GROUPS_PER_PAIR, STATE, SSM_GROUP)
    pair = jnp.arange(N_PAIRS)[:, None]
    member = jnp.arange(GROUPS_PER_PAIR)[None, :]
    local_group = (GROUPS_PER_PAIR * pair) % groups_per_block + member
    onehot = (local_group[..., None] == jnp.arange(groups_per_block)).astype(F32)
    wb = jnp.einsum('rkjph,kjg->kghrjp', bb, onehot).reshape(N_PAIRS, LANES, 2 * LANES)
    wb = wb.reshape(N_LANE_BLOCKS, PAIRS_PER_LANE_BLOCK, LANES, 2 * LANES).transpose(0, 2, 1, 3)
    wb = wb.reshape(N_LANE_BLOCKS, LANES, PAIRS_PER_LANE_BLOCK * 2 * LANES)
    cc = jnp.stack([c_re, -c_im]).reshape(
        2, N_LANE_BLOCKS, PAIRS_PER_LANE_BLOCK, GROUPS_PER_PAIR, SSM_GROUP, STATE)
    wc = jnp.einsum('rcqjhp,qa,jb->cqrjpabh', cc, jnp.eye(PAIRS_PER_LANE_BLOCK, dtype=F32),
                    jnp.eye(GROUPS_PER_PAIR, dtype=F32)).reshape(N_LANE_BLOCKS, D_MODEL, LANES)
    a_re_t = jnp.broadcast_to(ab_re.reshape(N_PAIRS, 1, LANES), (N_PAIRS, batch, LANES))
    a_im_t = jnp.broadcast_to(ab_im.reshape(N_PAIRS, 1, LANES), (N_PAIRS, batch, LANES))
    return wb.astype(BF16), wc.astype(BF16), a_re_t, a_im_t


def _s5_mixer(x, norm_w, w_in, a_re, a_im, log_dt, b_re, b_im, c_re, c_im, d_skip, w_glu,
              cast_weights=()):
    batch, seq, _ = x.shape
    assert batch == SUBLANES, "the scan keeps one timestep of all batches in one sublane tile"
    tt = S5_TT
    rows = batch * tt
    n_steps = seq // tt
    wb, wc, a_re_t, a_im_t = _s5_tables(a_re, a_im, log_dt, b_re, b_im, c_re, c_im, batch)
    rider_in, rider_out, rider_shapes, rider_args = _cast_riders(cast_weights, n_steps, lambda i: i)
    kern = functools.partial(_s5_kernel, batch=batch, tt=tt, n_riders=len(rider_args))
    out, *cast = pl.pallas_call(
        kern,
        out_shape=[jax.ShapeDtypeStruct(x.shape, F32)] + rider_shapes,
        grid=(n_steps,),
        in_specs=[
            pl.BlockSpec((batch, tt, D_MODEL), lambda i: (0, i, 0)),
            _resident((1, D_MODEL)),
            _resident((D_MODEL, D_MODEL)),
            _resident((N_LANE_BLOCKS, LANES, PAIRS_PER_LANE_BLOCK * 2 * LANES)),
            _resident((N_LANE_BLOCKS, D_MODEL, LANES)),
            _resident((N_PAIRS, batch, LANES)),
            _resident((N_PAIRS, batch, LANES)),
            _resident((1, D_MODEL)),
            _resident((D_MODEL, 2 * D_MODEL)),
        ] + rider_in,
        out_specs=[pl.BlockSpec((batch, tt, D_MODEL), lambda i: (0, i, 0))] + rider_out,
        scratch_shapes=[
            pltpu.VMEM((N_LANE_BLOCKS, rows, LANES), F32),
            pltpu.VMEM((N_PAIRS, rows, LANES), F32),
            pltpu.VMEM((N_PAIRS, rows, LANES), F32),
            pltpu.VMEM((N_LANE_BLOCKS, rows, LANES), F32),
            pltpu.VMEM((rows, D_MODEL), BF16),
            pltpu.VMEM((N_PAIRS, batch, LANES), F32),
            pltpu.VMEM((N_PAIRS, batch, LANES), F32),
        ],
        compiler_params=pltpu.CompilerParams(
            dimension_semantics=("arbitrary",), vmem_limit_bytes=VMEM_LIMIT),
        name="s5_mixer",
    )(x, norm_w.reshape(1, D_MODEL), w_in.astype(BF16), wb, wc, a_re_t, a_im_t,
      d_skip.reshape(1, D_MODEL), w_glu.astype(BF16), *rider_args)
    return out, cast


def _ffn_kernel(*refs, tl, pre_proj, final_norm, n_riders):
    refs = list(refs)
    x_ref = refs.pop(0)
    if pre_proj:
        a_ref, wo_ref = refs.pop(0), refs.pop(0)
    nw_ref, wup_ref, cw_ref, cb_ref, wd_ref = (refs.pop(0) for _ in range(5))
    if final_norm:
        nf_ref = refs.pop(0)
    rider_src = [refs.pop(0) for _ in range(n_riders)]
    o_ref = refs.pop(0)
    rider_dst = [refs.pop(0) for _ in range(n_riders)]
    gs_scr, act_scr = refs
    _run_cast_riders(rider_src, rider_dst)

    @pl.when(pl.program_id(1) == 0)
    def _():
        gs_scr[0:SUBLANES, :] = jnp.zeros((SUBLANES, D_FF), F32)

    x = x_ref[...]
    if pre_proj:
        x = x + jnp.dot(a_ref[...], wo_ref[...], preferred_element_type=F32)
    h = _rms(x, nw_ref[...]).astype(BF16)
    gu = jnp.dot(h, wup_ref[...], preferred_element_type=F32)
    for c in range(D_FF // FF_CHUNK):
        sl = slice(c * FF_CHUNK, (c + 1) * FF_CHUNK)
        g = gu[:, sl]
        up = gu[:, D_FF + c * FF_CHUNK:D_FF + (c + 1) * FF_CHUNK]
        gs_scr[SUBLANES:SUBLANES + tl, sl] = g
        g1 = gs_scr[SUBLANES - 1:SUBLANES - 1 + tl, sl]
        g2 = gs_scr[SUBLANES - 2:SUBLANES - 2 + tl, sl]
        gc = cw_ref[0:1, sl] * g2 + cw_ref[1:2, sl] * g1 + cw_ref[2:3, sl] * g + cb_ref[:, sl]
        act_scr[:, sl] = (gc * jax.nn.sigmoid(gc) * up).astype(BF16)
        gs_scr[0:SUBLANES, sl] = g[tl - SUBLANES:tl]
    y = x + jnp.dot(act_scr[...], wd_ref[...], preferred_element_type=F32)
    if final_norm:
        y = _rms(y, nf_ref[...])
    o_ref[...] = y


def _conv_ffn(x, norm_w, w_up, conv_w, conv_b, w_down, attn=None, w_o=None, final_w=None,
              cast_weights=()):
    batch, seq, _ = x.shape
    tl = FFN_TL
    steps_per_batch = seq // tl
    pre_proj = attn is not None
    final_norm = final_w is not None
    row_spec = pl.BlockSpec((None, tl, D_MODEL), lambda b, i: (b, i, 0))
    in_specs = [row_spec]
    args = [x]
    if pre_proj:
        in_specs += [row_spec, _resident((D_MODEL, D_MODEL))]
        args += [attn, w_o.astype(BF16)]
    in_specs += [_resident((1, D_MODEL)), _resident((D_MODEL, 2 * D_FF)), _resident((3, D_FF)),
                 _resident((1, D_FF)), _resident((D_FF, D_MODEL))]
    args += [norm_w.reshape(1, D_MODEL), w_up.astype(BF16), conv_w.reshape(3, D_FF),
             conv_b.reshape(1, D_FF), w_down.astype(BF16)]
    if final_norm:
        in_specs.append(_resident((1, D_MODEL)))
        args.append(final_w.reshape(1, D_MODEL))
    rider_in, rider_out, rider_shapes, rider_args = _cast_riders(
        cast_weights, batch * steps_per_batch, lambda b, i: b * steps_per_batch + i)
    kern = functools.partial(_ffn_kernel, tl=tl, pre_proj=pre_proj, final_norm=final_norm,
                             n_riders=len(rider_args))
    out, *cast = pl.pallas_call(
        kern,
        out_shape=[jax.ShapeDtypeStruct(x.shape, F32)] + rider_shapes,
        grid=(batch, steps_per_batch),
        in_specs=in_specs + rider_in,
        out_specs=[row_spec] + rider_out,
        scratch_shapes=[
            pltpu.VMEM((SUBLANES + tl, D_FF), F32),
            pltpu.VMEM((tl, D_FF), BF16),
        ],
        compiler_params=pltpu.CompilerParams(
            dimension_semantics=("arbitrary", "arbitrary"), vmem_limit_bytes=VMEM_LIMIT),
        name="conv_ffn",
    )(*args, *rider_args)
    return out, cast


def _qkv_kernel(x_ref, nq_ref, nkv_ref, wq_ref, wkv_ref, q_ref, k_ref, v_ref):
    x = x_ref[...]
    hq = _rms(x, nq_ref[...]).astype(BF16)
    hkv = _rms(x, nkv_ref[...]).astype(BF16)
    q = jnp.dot(hq, wq_ref[...], preferred_element_type=F32) * Q_SCALE
    kv = jnp.dot(hkv, wkv_ref[...], preferred_element_type=F32)
    q_ref[...] = q.astype(BF16)
    k_ref[...] = kv[:, :D_MODEL].astype(BF16)
    v_ref[...] = kv[:, D_MODEL:].astype(BF16)


def _qkv_proj(x, norm_q, norm_kv, w_q, w_kv):
    batch, seq, _ = x.shape
    tl = FFN_TL
    row_spec = pl.BlockSpec((None, tl, D_MODEL), lambda b, i: (b, i, 0))
    out = jax.ShapeDtypeStruct(x.shape, BF16)
    return pl.pallas_call(
        _qkv_kernel,
        out_shape=(out, out, out),
        grid=(batch, seq // tl),
        in_specs=[row_spec, _resident((1, D_MODEL)), _resident((1, D_MODEL)),
                  _resident((D_MODEL, D_MODEL)), _resident((D_MODEL, 2 * D_MODEL))],
        out_specs=(row_spec, row_spec, row_spec),
        compiler_params=pltpu.CompilerParams(
            dimension_semantics=("arbitrary", "arbitrary"), vmem_limit_bytes=VMEM_LIMIT),
        name="qkv_proj",
    )(x, norm_q.reshape(1, D_MODEL), norm_kv.reshape(1, D_MODEL), w_q.astype(BF16),
      w_kv.astype(BF16))


def _softplus2(z):
    return jnp.where(z > SOFTPLUS2_LINEAR_ABOVE, z, jnp.log2(1.0 + jnp.exp2(z)))


def _attn_kernel(q_ref, k_ref, v_ref, tri_ref, o_ref, z_scr, sp_scr, rs_scr, acc_scr, rem_scr,
                 *, bq, bk, n_sub):
    lane = lax.broadcasted_iota(jnp.int32, (bq, LANES), 1)
    first = lane < HEAD_DIM

    acc_scr[...] = jnp.zeros_like(acc_scr)
    rem_scr[...] = jnp.zeros_like(rem_scr)

    def walker(sub):
        qb = pl.program_id(2) * n_sub + sub
        n_blocks = qb + 1
        q2 = q_ref[sub * bq:(sub + 1) * bq, :]
        zero = jnp.zeros_like(q2)
        q_rows = jnp.concatenate([jnp.where(first, q2, zero), jnp.where(first, zero, q2)], axis=0)
        z_sub, sp_sub, rs_sub = z_scr.at[sub], sp_scr.at[sub], rs_scr.at[sub]
        acc_sub, rem_sub = acc_scr.at[sub], rem_scr.at[sub]

        def stage1(j, slot, diagonal=False, valid=None):
            start = pl.multiple_of(j * bk, bk)
            kb = k_ref[pl.ds(start, bk), :]
            if diagonal:
                row = lax.broadcasted_iota(jnp.int32, (bq, bk), 0)
                col = lax.broadcasted_iota(jnp.int32, (bq, bk), 1)
                keep = col < row
            elif valid is not None:
                keep = valid
            else:
                keep = None
            if diagonal:
                keep = jnp.concatenate([keep, keep], axis=0)
            z = lax.dot_general(q_rows, kb, (((1,), (1,)), ((), ())), preferred_element_type=F32)
            if keep is not None:
                z = jnp.where(keep, z, MASKED_SCORE)
            sp = _softplus2(z)
            z_sub[slot] = z.reshape(2, bq, bk)
            sp_sub[slot] = sp.astype(BF16).reshape(2, bq, bk)
            rs_sub[slot] = jnp.broadcast_to(
                jnp.sum(sp, axis=-1, keepdims=True), (2 * bq, LANES)).reshape(2, bq, LANES)

        def stage2(j, slot):
            start = pl.multiple_of(j * bk, bk)
            vb = v_ref[pl.ds(start, bk), :]
            cs = jnp.dot(sp_sub[slot].reshape(2 * bq, bk), tri_ref[...],
                         preferred_element_type=F32)
            rem = rem_sub[...].reshape(2 * bq, LANES)
            logw = jnp.minimum(z_sub[slot].reshape(2 * bq, bk) - cs, 0.0)
            w = jnp.exp2(logw - jnp.concatenate([rem] * (bk // LANES), axis=1))
            pv = jnp.dot(w.astype(BF16), vb, preferred_element_type=F32)
            acc_sub[...] += pv.reshape(2, bq, LANES)
            rem_sub[...] = (rem + rs_sub[slot].reshape(2 * bq, LANES)).reshape(2, bq, LANES)

        def blocks_beyond_are_dead(slot=None):
            r = rem_sub[...]
            if slot is not None:
                r = r + rs_sub[slot]
            return jnp.min(r) >= LOG2_W_UNDERFLOW

        def first_part():
            j1 = jnp.maximum(n_blocks - 2, 0)
            stage1(n_blocks - 1, 0, diagonal=True)
            stage2(n_blocks - 1, 0)
            stage1(j1, 1, valid=n_blocks > 1)
            stage2(j1, 1)
            return jnp.logical_and(n_blocks > 2, jnp.logical_not(blocks_beyond_are_dead()))

        def rest(more_blocks_live):
            @pl.when(more_blocks_live)
            def _():
                stage1(n_blocks - 3, 0)

                def cond(carry):
                    p, dead = carry
                    return jnp.logical_and(p < n_blocks - 1, jnp.logical_not(dead))

                def body(carry):
                    p, _ = carry
                    slot = p % 2
                    stage2(n_blocks - 1 - p, slot)
                    stage1(n_blocks - 2 - p, 1 - slot)
                    return p + 1, blocks_beyond_are_dead(slot=1 - slot)

                p_last, _ = lax.while_loop(cond, body,
                                           (jnp.int32(2), blocks_beyond_are_dead(slot=0)))
                stage2(n_blocks - 1 - p_last, p_last % 2)

        return first_part, rest

    walkers = [walker(sub) for sub in range(n_sub)]
    live = [first_part() for first_part, _ in walkers]
    for (_, rest), more_blocks_live in zip(walkers, live):
        rest(more_blocks_live)
    for sub in range(n_sub):
        o_ref[sub * bq:(sub + 1) * bq, :] = jnp.where(
            first, acc_scr[sub, 0], acc_scr[sub, 1]).astype(BF16)


def _sb_attention(q, k, v):
    batch, seq, _ = q.shape
    bq, bk, n_sub = ATT_BQ, ATT_BK, ATT_Q_BLOCKS_PER_STEP
    assert bq == bk, "exactly the first key block touches the diagonal"
    tri = (jnp.arange(bk)[:, None] >= jnp.arange(bk)[None, :]).astype(BF16)
    kern = functools.partial(_attn_kernel, bq=bq, bk=bk, n_sub=n_sub)
    kv_spec = pl.BlockSpec((None, seq, LANES), lambda b, hp, i: (b, 0, hp))
    q_spec = pl.BlockSpec((None, n_sub * bq, LANES), lambda b, hp, i: (b, i, hp))
    return pl.pallas_call(
        kern,
        out_shape=jax.ShapeDtypeStruct(q.shape, BF16),
        grid=(batch, N_HEADS * HEAD_DIM // LANES, seq // (n_sub * bq)),
        in_specs=[q_spec, kv_spec, kv_spec, _resident((bk, bk))],
        out_specs=q_spec,
        scratch_shapes=[
            pltpu.VMEM((n_sub, 2, 2, bq, bk), F32),
            pltpu.VMEM((n_sub, 2, 2, bq, bk), BF16),
            pltpu.VMEM((n_sub, 2, 2, bq, LANES), F32),
            pltpu.VMEM((n_sub, 2, bq, LANES), F32),
            pltpu.VMEM((n_sub, 2, bq, LANES), F32),
        ],
        compiler_params=pltpu.CompilerParams(
            dimension_semantics=("arbitrary", "arbitrary", "arbitrary"),
            vmem_limit_bytes=VMEM_LIMIT),
        name="sb_attn",
    )(q, k, v, tri)


def kernel(x, norm_mix, norm_ffn, norm_kv, norm_final, ssm_w_in, ssm_a_re, ssm_a_im, ssm_log_dt, ssm_b_re, ssm_b_im, ssm_c_re, ssm_c_im, ssm_d, ssm_w_glu, kv_w, attn_w_q, attn_w_o, ffn_w_up, ffn_conv_w, ffn_conv_b, ffn_w_down):
    depth = norm_mix.shape[0]
    n_a = ssm_w_in.shape[0]
    assert depth == 2 and n_a == 1 and attn_w_q.shape[0] == 1
    x, (w_up0, w_down0, w_q, w_kv) = _s5_mixer(
        x, norm_mix[0], ssm_w_in[0], ssm_a_re[0], ssm_a_im[0], ssm_log_dt[0], ssm_b_re[0],
        ssm_b_im[0], ssm_c_re[0], ssm_c_im[0], ssm_d[0], ssm_w_glu[0],
        cast_weights=[(ffn_w_up, 0), (ffn_w_down, 0), (attn_w_q, 0), (kv_w[None], 0)])
    x, (w_up1, w_down1, w_o) = _conv_ffn(
        x, norm_ffn[0], w_up0, ffn_conv_w[0], ffn_conv_b[0], w_down0,
        cast_weights=[(ffn_w_up, 1), (ffn_w_down, 1), (attn_w_o, 0)])
    q, k, v = _qkv_proj(x, norm_mix[1], norm_kv, w_q, w_kv)
    attn = _sb_attention(q, k, v)
    out, _ = _conv_ffn(x, norm_ffn[1], w_up1, ffn_conv_w[1], ffn_conv_b[1], w_down1,
                       attn=attn, w_o=w_o, final_w=norm_final)
    return out
```

```python
import functools
import math

import jax
import jax.numpy as jnp
from jax import lax
from jax.experimental import pallas as pl
from jax.experimental.pallas import tpu as pltpu

F32 = jnp.float32
BF16 = jnp.bfloat16

D_MODEL = 1024
N_GROUPS = 64
SSM_GROUP = 16
STATE = 64
N_HEADS = 16
HEAD_DIM = 64
D_FF = 2816
EPS = 1e-6

LANES = 128
SUBLANES = 8
N_LANE_BLOCKS = D_MODEL // LANES
GROUPS_PER_PAIR = 2
N_PAIRS = N_GROUPS // GROUPS_PER_PAIR
PAIRS_PER_LANE_BLOCK = N_PAIRS // N_LANE_BLOCKS

S5_TT = 64
GLU_CHUNK = 256
FFN_TL = 512
FF_CHUNK = 256
ATT_BQ = 256
ATT_BK = 256
ATT_Q_BLOCKS_PER_STEP = 8
Q_SCALE = HEAD_DIM ** -0.5 * math.log2(math.e)
MASKED_SCORE = -1e30
SOFTPLUS2_LINEAR_ABOVE = 32.0
LOG2_W_UNDERFLOW = 150.0
VMEM_LIMIT = 56 * 1024 * 1024


def _rms(x, g):
    ms = jnp.mean(x * x, axis=-1, keepdims=True)
    return x * lax.rsqrt(ms + EPS) * g


def _resident(shape):
    nd = len(shape)
    return pl.BlockSpec(shape, lambda *_: (0,) * nd, pipeline_mode=pl.Buffered(1))


BF16_SUBLANES = 16
CAST_BLOCK_ROWS = 128


def _cast_riders(weights, n_steps, step_of):
    in_specs, out_specs, out_shapes, args = [], [], [], []
    for stacked, layer in weights:
        _, n_rows, n_cols = stacked.shape
        rows = n_rows // n_steps
        if n_rows % n_steps or rows % BF16_SUBLANES:
            rows = CAST_BLOCK_ROWS
        n_blocks = n_rows // rows
        assert n_rows % rows == 0 and n_blocks <= n_steps

        def block(*grid_idx, n_blocks=n_blocks):
            return jnp.minimum(step_of(*grid_idx), n_blocks - 1)

        in_specs.append(pl.BlockSpec(
            (None, rows, n_cols), lambda *g, layer=layer, block=block: (layer, block(*g), 0)))
        out_specs.append(pl.BlockSpec((rows, n_cols), lambda *g, block=block: (block(*g), 0)))
        out_shapes.append(jax.ShapeDtypeStruct((n_rows, n_cols), BF16))
        args.append(stacked)
    return in_specs, out_specs, out_shapes, args


def _run_cast_riders(src_refs, dst_refs):
    for src, dst in zip(src_refs, dst_refs):
        dst[...] = src[...].astype(BF16)


def _s5_kernel(*refs, batch, tt, n_riders):
    x_ref, nw_ref, win_ref, wb_ref, wc_ref, are_ref, aim_ref, d_ref, wglu_ref = refs[:9]
    refs = refs[9:]
    rider_src, o_ref, rider_dst = refs[:n_riders], refs[n_riders], refs[n_riders + 1:2 * n_riders + 1]
    utb_scr, bre_scr, bim_scr, ytb_scr, yg_scr, sre_scr, sim_scr = refs[2 * n_riders + 1:]
    rows = batch * tt
    _run_cast_riders(rider_src, rider_dst)

    @pl.when(pl.program_id(0) == 0)
    def _():
        sre_scr[...] = jnp.zeros_like(sre_scr)
        sim_scr[...] = jnp.zeros_like(sim_scr)

    xb = x_ref[...].reshape(rows, D_MODEL)
    h = _rms(xb, nw_ref[...]).astype(BF16)
    u = jnp.dot(h, win_ref[...], preferred_element_type=F32)

    for c in range(N_LANE_BLOCKS):
        for b in range(batch):
            utb_scr[c, pl.ds(b, tt, stride=batch), :] = u[b * tt:(b + 1) * tt, c * LANES:(c + 1) * LANES]

    def bproj(c):
        bu = jnp.dot(utb_scr[c].astype(BF16), wb_ref[c], preferred_element_type=F32)
        for q in range(PAIRS_PER_LANE_BLOCK):
            k = c * PAIRS_PER_LANE_BLOCK + q
            bre_scr[k] = bu[:, 2 * q * LANES:(2 * q + 1) * LANES]
            bim_scr[k] = bu[:, (2 * q + 1) * LANES:(2 * q + 2) * LANES]

    def scan(c):
        for k in range(c * PAIRS_PER_LANE_BLOCK, (c + 1) * PAIRS_PER_LANE_BLOCK):
            ar, ai = are_ref[k], aim_ref[k]
            sr, si = sre_scr[k], sim_scr[k]
            for t in range(tt):
                rows_t = pl.ds(t * SUBLANES, SUBLANES)
                nr = ar * sr - ai * si + bre_scr[k, rows_t, :]
                ni = ar * si + ai * sr + bim_scr[k, rows_t, :]
                bre_scr[k, rows_t, :] = nr
                bim_scr[k, rows_t, :] = ni
                sr, si = nr, ni
            sre_scr[k] = sr
            sim_scr[k] = si

    def cproj(c):
        parts = []
        for q in range(PAIRS_PER_LANE_BLOCK):
            k = c * PAIRS_PER_LANE_BLOCK + q
            parts += [bre_scr[k].astype(BF16), bim_scr[k].astype(BF16)]
        lhs = jnp.concatenate(parts, axis=1)
        y = jnp.dot(lhs, wc_ref[c], preferred_element_type=F32)
        y = y + d_ref[:, c * LANES:(c + 1) * LANES] * utb_scr[c]
        ytb_scr[c] = jax.nn.gelu(y)
        for b in range(batch):
            yg_scr[pl.ds(b * tt, tt), c * LANES:(c + 1) * LANES] = (
                ytb_scr[c, pl.ds(b, tt, stride=batch), :].astype(BF16))

    for c in range(N_LANE_BLOCKS + 2):
        if c < N_LANE_BLOCKS:
            bproj(c)
        if 1 <= c <= N_LANE_BLOCKS:
            scan(c - 1)
        if c >= 2:
            cproj(c - 2)

    yg = yg_scr[...]
    for c in range(D_MODEL // GLU_CHUNK):
        sl = slice(c * GLU_CHUNK, (c + 1) * GLU_CHUNK)
        val = jnp.dot(yg, wglu_ref[:, sl], preferred_element_type=F32)
        gate = jnp.dot(yg, wglu_ref[:, D_MODEL + c * GLU_CHUNK:D_MODEL + (c + 1) * GLU_CHUNK],
                       preferred_element_type=F32)
        out = x_ref[:, :, sl].reshape(rows, GLU_CHUNK) + val * jax.nn.sigmoid(gate)
        o_ref[:, :, sl] = out.reshape(batch, tt, GLU_CHUNK)


def _s5_tables(a_re, a_im, log_dt, b_re, b_im, c_re, c_im, batch):
    dt = jnp.exp(log_dt)[:, None]
    mag = jnp.exp(a_re * dt)
    ab_re = mag * jnp.cos(a_im * dt)
    ab_im = mag * jnp.sin(a_im * dt)
    den = a_re * a_re + a_im * a_im
    f_re = ((ab_re - 1.0) * a_re + ab_im * a_im) / den
    f_im = (ab_im * a_re - (ab_re - 1.0) * a_im) / den
    bb_re = f_re[..., None] * b_re - f_im[..., None] * b_im
    bb_im = f_re[..., None] * b_im + f_im[..., None] * b_re
    groups_per_block = N_GROUPS // N_LANE_BLOCKS
    bb = jnp.stack([bb_re, bb_im]).reshape(2, N_PAIRS, GROUPS_PER_PAIR, STATE, SSM_GROUP)
    pair = jnp.arange(N_PAIRS)[:, None]
    member = jnp.arange(GROUPS_PER_PAIR)[None, :]
    local_group = (GROUPS_PER_PAIR * pair) % groups_per_block + member
    onehot = (local_group[..., None] == jnp.arange(groups_per_block)).astype(F32)
    wb = jnp.einsum('rkjph,kjg->kghrjp', bb, onehot).reshape(N_PAIRS, LANES, 2 * LANES)
    wb = wb.reshape(N_LANE_BLOCKS, PAIRS_PER_LANE_BLOCK, LANES, 2 * LANES).transpose(0, 2, 1, 3)
    wb = wb.reshape(N_LANE_BLOCKS, LANES, PAIRS_PER_LANE_BLOCK * 2 * LANES)
    cc = jnp.stack([c_re, -c_im]).reshape(
        2, N_LANE_BLOCKS, PAIRS_PER_LANE_BLOCK, GROUPS_PER_PAIR, SSM_GROUP, STATE)
    wc = jnp.einsum('rcqjhp,qa,jb->cqrjpabh', cc, jnp.eye(PAIRS_PER_LANE_BLOCK, dtype=F32),
                    jnp.eye(GROUPS_PER_PAIR, dtype=F32)).reshape(N_LANE_BLOCKS, D_MODEL, LANES)
    a_re_t = jnp.broadcast_to(ab_re.reshape(N_PAIRS, 1, LANES), (N_PAIRS, batch, LANES))
    a_im_t = jnp.broadcast_to(ab_im.reshape(N_PAIRS, 1, LANES), (N_PAIRS, batch, LANES))
    return wb.astype(BF16), wc.astype(BF16), a_re_t, a_im_t


def _s5_mixer(x, norm_w, w_in, a_re, a_im, log_dt, b_re, b_im, c_re, c_im, d_skip, w_glu,
              cast_weights=()):
    batch, seq, _ = x.shape
    assert batch == SUBLANES, "the scan keeps one timestep of all batches in one sublane tile"
    tt = S5_TT
    rows = batch * tt
    n_steps = seq // tt
    wb, wc, a_re_t, a_im_t = _s5_tables(a_re, a_im, log_dt, b_re, b_im, c_re, c_im, batch)
    rider_in, rider_out, rider_shapes, rider_args = _cast_riders(cast_weights, n_steps, lambda i: i)
    kern = functools.partial(_s5_kernel, batch=batch, tt=tt, n_riders=len(rider_args))
    out, *cast = pl.pallas_call(
        kern,
        out_shape=[jax.ShapeDtypeStruct(x.shape, F32)] + rider_shapes,
        grid=(n_steps,),
        in_specs=[
            pl.BlockSpec((batch, tt, D_MODEL), lambda i: (0, i, 0)),
            _resident((1, D_MODEL)),
            _resident((D_MODEL, D_MODEL)),
            _resident((N_LANE_BLOCKS, LANES, PAIRS_PER_LANE_BLOCK * 2 * LANES)),
            _resident((N_LANE_BLOCKS, D_MODEL, LANES)),
            _resident((N_PAIRS, batch, LANES)),
            _resident((N_PAIRS, batch, LANES)),
            _resident((1, D_MODEL)),
            _resident((D_MODEL, 2 * D_MODEL)),
        ] + rider_in,
        out_specs=[pl.BlockSpec((batch, tt, D_MODEL), lambda i: (0, i, 0))] + rider_out,
        scratch_shapes=[
            pltpu.VMEM((N_LANE_BLOCKS, rows, LANES), F32),
            pltpu.VMEM((N_PAIRS, rows, LANES), F32),
            pltpu.VMEM((N_PAIRS, rows, LANES), F32),
            pltpu.VMEM((N_LANE_BLOCKS, rows, LANES), F32),
            pltpu.VMEM((rows, D_MODEL), BF16),
            pltpu.VMEM((N_PAIRS, batch, LANES), F32),
            pltpu.VMEM((N_PAIRS, batch, LANES), F32),
        ],
        compiler_params=pltpu.CompilerParams(
            dimension_semantics=("arbitrary",), vmem_limit_bytes=VMEM_LIMIT),
        name="s5_mixer",
    )(x, norm_w.reshape(1, D_MODEL), w_in.astype(BF16), wb, wc, a_re_t, a_im_t,
      d_skip.reshape(1, D_MODEL), w_glu.astype(BF16), *rider_args)
    return out, cast


def _ffn_kernel(*refs, tl, pre_proj, final_norm, n_riders):
    refs = list(refs)
    x_ref = refs.pop(0)
    if pre_proj:
        a_ref, wo_ref = refs.pop(0), refs.pop(0)
    nw_ref, wup_ref, cw_ref, cb_ref, wd_ref = (refs.pop(0) for _ in range(5))
    if final_norm:
        nf_ref = refs.pop(0)
    rider_src = [refs.pop(0) for _ in range(n_riders)]
    o_ref = refs.pop(0)
    rider_dst = [refs.pop(0) for _ in range(n_riders)]
    gs_scr, act_scr = refs
    _run_cast_riders(rider_src, rider_dst)

    @pl.when(pl.program_id(1) == 0)
    def _():
        gs_scr[0:SUBLANES, :] = jnp.zeros((SUBLANES, D_FF), F32)

    x = x_ref[...]
    if pre_proj:
        x = x + jnp.dot(a_ref[...], wo_ref[...], preferred_element_type=F32)
    h = _rms(x, nw_ref[...]).astype(BF16)
    gu = jnp.dot(h, wup_ref[...], preferred_element_type=F32)
    for c in range(D_FF // FF_CHUNK):
        sl = slice(c * FF_CHUNK, (c + 1) * FF_CHUNK)
        g = gu[:, sl]
        up = gu[:, D_FF + c * FF_CHUNK:D_FF + (c + 1) * FF_CHUNK]
        gs_scr[SUBLANES:SUBLANES + tl, sl] = g
        g1 = gs_scr[SUBLANES - 1:SUBLANES - 1 + tl, sl]
        g2 = gs_scr[SUBLANES - 2:SUBLANES - 2 + tl, sl]
        gc = cw_ref[0:1, sl] * g2 + cw_ref[1:2, sl] * g1 + cw_ref[2:3, sl] * g + cb_ref[:, sl]
        act_scr[:, sl] = (gc * jax.nn.sigmoid(gc) * up).astype(BF16)
        gs_scr[0:SUBLANES, sl] = g[tl - SUBLANES:tl]
    y = x + jnp.dot(act_scr[...], wd_ref[...], preferred_element_type=F32)
    if final_norm:
        y = _rms(y, nf_ref[...])
    o_ref[...] = y


def _conv_ffn(x, norm_w, w_up, conv_w, conv_b, w_down, attn=None, w_o=None, final_w=None,
              cast_weights=()):
    batch, seq, _ = x.shape
    tl = FFN_TL
    steps_per_batch = seq // tl
    pre_proj = attn is not None
    final_norm = final_w is not None
    row_spec = pl.BlockSpec((None, tl, D_MODEL), lambda b, i: (b, i, 0))
    in_specs = [row_spec]
    args = [x]
    if pre_proj:
        in_specs += [row_spec, _resident((D_MODEL, D_MODEL))]
        args += [attn, w_o.astype(BF16)]
    in_specs += [_resident((1, D_MODEL)), _resident((D_MODEL, 2 * D_FF)), _resident((3, D_FF)),
                 _resident((1, D_FF)), _resident((D_FF, D_MODEL))]
    args += [norm_w.reshape(1, D_MODEL), w_up.astype(BF16), conv_w.reshape(3, D_FF),
             conv_b.reshape(1, D_FF), w_down.astype(BF16)]
    if final_norm:
        in_specs.append(_resident((1, D_MODEL)))
        args.append(final_w.reshape(1, D_MODEL))
    rider_in, rider_out, rider_shapes, rider_args = _cast_riders(
        cast_weights, batch * steps_per_batch, lambda b, i: b * steps_per_batch + i)
    kern = functools.partial(_ffn_kernel, tl=tl, pre_proj=pre_proj, final_norm=final_norm,
                             n_riders=len(rider_args))
    out, *cast = pl.pallas_call(
        kern,
        out_shape=[jax.ShapeDtypeStruct(x.shape, F32)] + rider_shapes,
        grid=(batch, steps_per_batch),
        in_specs=in_specs + rider_in,
        out_specs=[row_spec] + rider_out,
        scratch_shapes=[
            pltpu.VMEM((SUBLANES + tl, D_FF), F32),
            pltpu.VMEM((tl, D_FF), BF16),
        ],
        compiler_params=pltpu.CompilerParams(
            dimension_semantics=("arbitrary", "arbitrary"), vmem_limit_bytes=VMEM_LIMIT),
        name="conv_ffn",
    )(*args, *rider_args)
    return out, cast


def _qkv_kernel(x_ref, nq_ref, nkv_ref, wq_ref, wkv_ref, q_ref, k_ref, v_ref):
    x = x_ref[...]
    hq = _rms(x, nq_ref[...]).astype(BF16)
    hkv = _rms(x, nkv_ref[...]).astype(BF16)
    q = jnp.dot(hq, wq_ref[...], preferred_element_type=F32) * Q_SCALE
    kv = jnp.dot(hkv, wkv_ref[...], preferred_element_type=F32)
    q_ref[...] = q.astype(BF16)
    k_ref[...] = kv[:, :D_MODEL].astype(BF16)
    v_ref[...] = kv[:, D_MODEL:].astype(BF16)


def _qkv_proj(x, norm_q, norm_kv, w_q, w_kv):
    batch, seq, _ = x.shape
    tl = FFN_TL
    row_spec = pl.BlockSpec((None, tl, D_MODEL), lambda b, i: (b, i, 0))
    out = jax.ShapeDtypeStruct(x.shape, BF16)
    return pl.pallas_call(
        _qkv_kernel,
        out_shape=(out, out, out),
        grid=(batch, seq // tl),
        in_specs=[row_spec, _resident((1, D_MODEL)), _resident((1, D_MODEL)),
                  _resident((D_MODEL, D_MODEL)), _resident((D_MODEL, 2 * D_MODEL))],
        out_specs=(row_spec, row_spec, row_spec),
        compiler_params=pltpu.CompilerParams(
            dimension_semantics=("arbitrary", "arbitrary"), vmem_limit_bytes=VMEM_LIMIT),
        name="qkv_proj",
    )(x, norm_q.reshape(1, D_MODEL), norm_kv.reshape(1, D_MODEL), w_q.astype(BF16),
      w_kv.astype(BF16))


def _softplus2(z):
    return jnp.where(z > SOFTPLUS2_LINEAR_ABOVE, z, jnp.log2(1.0 + jnp.exp2(z)))


def _attn_kernel(q_ref, k_ref, v_ref, tri_ref, o_ref, z_scr, sp_scr, rs_scr, acc_scr, rem_scr,
                 *, bq, bk, n_sub):
    lane = lax.broadcasted_iota(jnp.int32, (bq, LANES), 1)
    first = lane < HEAD_DIM

    acc_scr[...] = jnp.zeros_like(acc_scr)
    rem_scr[...] = jnp.zeros_like(rem_scr)

    def walker(sub):
        qb = pl.program_id(2) * n_sub + sub
        n_blocks = qb + 1
        q2 = q_ref[sub * bq:(sub + 1) * bq, :]
        zero = jnp.zeros_like(q2)
        q_rows = jnp.concatenate([jnp.where(first, q2, zero), jnp.where(first, zero, q2)], axis=0)
        z_sub, sp_sub, rs_sub = z_scr.at[sub], sp_scr.at[sub], rs_scr.at[sub]
        acc_sub, rem_sub = acc_scr.at[sub], rem_scr.at[sub]

        def stage1(j, slot, diagonal=False, valid=None):
            start = pl.multiple_of(j * bk, bk)
            kb = k_ref[pl.ds(start, bk), :]
            if diagonal:
                row = lax.broadcasted_iota(jnp.int32, (bq, bk), 0)
                col = lax.broadcasted_iota(jnp.int32, (bq, bk), 1)
                keep = col < row
            elif valid is not None:
                keep = valid
            else:
                keep = None
            if diagonal:
                keep = jnp.concatenate([keep, keep], axis=0)
            z = lax.dot_general(q_rows, kb, (((1,), (1,)), ((), ())), preferred_element_type=F32)
            if keep is not None:
                z = jnp.where(keep, z, MASKED_SCORE)
            sp = _softplus2(z)
            z_sub[slot] = z.reshape(2, bq, bk)
            sp_sub[slot] = sp.astype(BF16).reshape(2, bq, bk)
            rs_sub[slot] = jnp.broadcast_to(
                jnp.sum(sp, axis=-1, keepdims=True), (2 * bq, LANES)).reshape(2, bq, LANES)

        def stage2(j, slot):
            start = pl.multiple_of(j * bk, bk)
            vb = v_ref[pl.ds(start, bk), :]
            cs = jnp.dot(sp_sub[slot].reshape(2 * bq, bk), tri_ref[...],
                         preferred_element_type=F32)
            rem = rem_sub[...].reshape(2 * bq, LANES)
            logw = jnp.minimum(z_sub[slot].reshape(2 * bq, bk) - cs, 0.0)
            w = jnp.exp2(logw - jnp.concatenate([rem] * (bk // LANES), axis=1))
            pv = jnp.dot(w.astype(BF16), vb, preferred_element_type=F32)
            acc_sub[...] += pv.reshape(2, bq, LANES)
            rem_sub[...] = (rem + rs_sub[slot].reshape(2 * bq, LANES)).reshape(2, bq, LANES)

        def blocks_beyond_are_dead(slot=None):
            r = rem_sub[...]
            if slot is not None:
                r = r + rs_sub[slot]
            return jnp.min(r) >= LOG2_W_UNDERFLOW

        def first_part():
            j1 = jnp.maximum(n_blocks - 2, 0)
            stage1(n_blocks - 1, 0, diagonal=True)
            stage2(n_blocks - 1, 0)
            stage1(j1, 1, valid=n_blocks > 1)
            stage2(j1, 1)
            return jnp.logical_and(n_blocks > 2, jnp.logical_not(blocks_beyond_are_dead()))

        def rest(more_blocks_live):
            @pl.when(more_blocks_live)
            def _():
                stage1(n_blocks - 3, 0)

                def cond(carry):
                    p, dead = carry
                    return jnp.logical_and(p < n_blocks - 1, jnp.logical_not(dead))

                def body(carry):
                    p, _ = carry
                    slot = p % 2
                    stage2(n_blocks - 1 - p, slot)
                    stage1(n_blocks - 2 - p, 1 - slot)
                    return p + 1, blocks_beyond_are_dead(slot=1 - slot)

                p_last, _ = lax.while_loop(cond, body,
                                           (jnp.int32(2), blocks_beyond_are_dead(slot=0)))
                stage2(n_blocks - 1 - p_last, p_last % 2)

        return first_part, rest

    walkers = [walker(sub) for sub in range(n_sub)]
    live = [first_part() for first_part, _ in walkers]
    for (_, rest), more_blocks_live in zip(walkers, live):
        rest(more_blocks_live)
    for sub in range(n_sub):
        o_ref[sub * bq:(sub + 1) * bq, :] = jnp.where(
            first, acc_scr[sub, 0], acc_scr[sub, 1]).astype(BF16)


def _sb_attention(q, k, v):
    batch, seq, _ = q.shape
    bq, bk, n_sub = ATT_BQ, ATT_BK, ATT_Q_BLOCKS_PER_STEP
    assert bq == bk, "exactly the first key block touches the diagonal"
    tri = (jnp.arange(bk)[:, None] >= jnp.arange(bk)[None, :]).astype(BF16)
    kern = functools.partial(_attn_kernel, bq=bq, bk=bk, n_sub=n_sub)
    kv_spec = pl.BlockSpec((None, seq, LANES), lambda b, hp, i: (b, 0, hp))
    q_spec = pl.BlockSpec((None, n_sub * bq, LANES), lambda b, hp, i: (b, i, hp))
    return pl.pallas_call(
        kern,
        out_shape=jax.ShapeDtypeStruct(q.shape, BF16),
        grid=(batch, N_HEADS * HEAD_DIM // LANES, seq // (n_sub * bq)),
        in_specs=[q_spec, kv_spec, kv_spec, _resident((bk, bk))],
        out_specs=q_spec,
        scratch_shapes=[
            pltpu.VMEM((n_sub, 2, 2, bq, bk), F32),
            pltpu.VMEM((n_sub, 2, 2, bq, bk), BF16),
            pltpu.VMEM((n_sub, 2, 2, bq, LANES), F32),
            pltpu.VMEM((n_sub, 2, bq, LANES), F32),
            pltpu.VMEM((n_sub, 2, bq, LANES), F32),
        ],
        compiler_params=pltpu.CompilerParams(
            dimension_semantics=("arbitrary", "arbitrary", "arbitrary"),
            vmem_limit_bytes=VMEM_LIMIT),
        name="sb_attn",
    )(q, k, v, tri)


def kernel(x, norm_mix, norm_ffn, norm_kv, norm_final, ssm_w_in, ssm_a_re, ssm_a_im, ssm_log_dt, ssm_b_re, ssm_b_im, ssm_c_re, ssm_c_im, ssm_d, ssm_w_glu, kv_w, attn_w_q, attn_w_o, ffn_w_up, ffn_conv_w, ffn_conv_b, ffn_w_down):
    depth = norm_mix.shape[0]
    n_a = ssm_w_in.shape[0]
    assert depth == 2 and n_a == 1 and attn_w_q.shape[0] == 1
    x, (w_up0, w_down0, w_q, w_kv) = _s5_mixer(
        x, norm_mix[0], ssm_w_in[0], ssm_a_re[0], ssm_a_im[0], ssm_log_dt[0], ssm_b_re[0],
        ssm_b_im[0], ssm_c_re[0], ssm_c_im[0], ssm_d[0], ssm_w_glu[0],
        cast_weights=[(ffn_w_up, 0), (ffn_w_down, 0), (attn_w_q, 0), (kv_w[None], 0)])
    x, (w_up1, w_down1, w_o) = _conv_ffn(
        x, norm_ffn[0], w_up0, ffn_conv_w[0], ffn_conv_b[0], w_down0,
        cast_weights=[(ffn_w_up, 1), (ffn_w_down, 1), (attn_w_o, 0)])
    q, k, v = _qkv_proj(x, norm_mix[1], norm_kv, w_q, w_kv)
    attn = _sb_attention(q, k, v)
    out, _ = _conv_ffn(x, norm_ffn[1], w_up1, ffn_conv_w[1], ffn_conv_b[1], w_down1,
                       attn=attn, w_o=w_o, final_w=norm_final)
    return out
```

```python
import functools
import math

import jax
import jax.numpy as jnp
from jax import lax
from jax.experimental import pallas as pl
from jax.experimental.pallas import tpu as pltpu

F32 = jnp.float32
BF16 = jnp.bfloat16

D_MODEL = 1024
N_GROUPS = 64
SSM_GROUP = 16
STATE = 64
N_HEADS = 16
HEAD_DIM = 64
D_FF = 2816
EPS = 1e-6

LANES = 128
SUBLANES = 8
N_LANE_BLOCKS = D_MODEL // LANES
GROUPS_PER_PAIR = 2
N_PAIRS = N_GROUPS // GROUPS_PER_PAIR
PAIRS_PER_LANE_BLOCK = N_PAIRS // N_LANE_BLOCKS

S5_TT = 64
GLU_CHUNK = 256
FFN_TL = 512
FF_CHUNK = 256
ATT_BQ = 256
ATT_BK = 256
ATT_Q_BLOCKS_PER_STEP = 16
Q_SCALE = HEAD_DIM ** -0.5 * math.log2(math.e)
MASKED_SCORE = -1e30
SOFTPLUS2_LINEAR_ABOVE = 32.0
LOG2_W_UNDERFLOW = 150.0
VMEM_LIMIT = 56 * 1024 * 1024


def _rms(x, g):
    ms = jnp.mean(x * x, axis=-1, keepdims=True)
    return x * lax.rsqrt(ms + EPS) * g


def _resident(shape):
    nd = len(shape)
    return pl.BlockSpec(shape, lambda *_: (0,) * nd, pipeline_mode=pl.Buffered(1))


BF16_SUBLANES = 16
CAST_BLOCK_ROWS = 128


def _cast_riders(weights, n_steps, step_of):
    in_specs, out_specs, out_shapes, args = [], [], [], []
    for stacked, layer in weights:
        _, n_rows, n_cols = stacked.shape
        rows = n_rows // n_steps
        if n_rows % n_steps or rows % BF16_SUBLANES:
            rows = CAST_BLOCK_ROWS
        n_blocks = n_rows // rows
        assert n_rows % rows == 0 and n_blocks <= n_steps

        def block(*grid_idx, n_blocks=n_blocks):
            return jnp.minimum(step_of(*grid_idx), n_blocks - 1)

        in_specs.append(pl.BlockSpec(
            (None, rows, n_cols), lambda *g, layer=layer, block=block: (layer, block(*g), 0)))
        out_specs.append(pl.BlockSpec((rows, n_cols), lambda *g, block=block: (block(*g), 0)))
        out_shapes.append(jax.ShapeDtypeStruct((n_rows, n_cols), BF16))
        args.append(stacked)
    return in_specs, out_specs, out_shapes, args


def _run_cast_riders(src_refs, dst_refs):
    for src, dst in zip(src_refs, dst_refs):
        dst[...] = src[...].astype(BF16)


def _s5_kernel(*refs, batch, tt, n_riders):
    x_ref, nw_ref, win_ref, wb_ref, wc_ref, are_ref, aim_ref, d_ref, wglu_ref = refs[:9]
    refs = refs[9:]
    rider_src, o_ref, rider_dst = refs[:n_riders], refs[n_riders], refs[n_riders + 1:2 * n_riders + 1]
    utb_scr, bre_scr, bim_scr, ytb_scr, yg_scr, sre_scr, sim_scr = refs[2 * n_riders + 1:]
    rows = batch * tt
    _run_cast_riders(rider_src, rider_dst)

    @pl.when(pl.program_id(0) == 0)
    def _():
        sre_scr[...] = jnp.zeros_like(sre_scr)
        sim_scr[...] = jnp.zeros_like(sim_scr)

    xb = x_ref[...].reshape(rows, D_MODEL)
    h = _rms(xb, nw_ref[...]).astype(BF16)
    u = jnp.dot(h, win_ref[...], preferred_element_type=F32)

    for c in range(N_LANE_BLOCKS):
        for b in range(batch):
            utb_scr[c, pl.ds(b, tt, stride=batch), :] = u[b * tt:(b + 1) * tt, c * LANES:(c + 1) * LANES]

    def bproj(c):
        bu = jnp.dot(utb_scr[c].astype(BF16), wb_ref[c], preferred_element_type=F32)
        for q in range(PAIRS_PER_LANE_BLOCK):
            k = c * PAIRS_PER_LANE_BLOCK + q
            bre_scr[k] = bu[:, 2 * q * LANES:(2 * q + 1) * LANES]
            bim_scr[k] = bu[:, (2 * q + 1) * LANES:(2 * q + 2) * LANES]

    def scan(c):
        for k in range(c * PAIRS_PER_LANE_BLOCK, (c + 1) * PAIRS_PER_LANE_BLOCK):
            ar, ai = are_ref[k], aim_ref[k]
            sr, si = sre_scr[k], sim_scr[k]
            for t in range(tt):
                rows_t = pl.ds(t * SUBLANES, SUBLANES)
                nr = ar * sr - ai * si + bre_scr[k, rows_t, :]
                ni = ar * si + ai * sr + bim_scr[k, rows_t, :]
                bre_scr[k, rows_t, :] = nr
                bim_scr[k, rows_t, :] = ni
                sr, si = nr, ni
            sre_scr[k] = sr
            sim_scr[k] = si

    def cproj(c):
        parts = []
        for q in range(PAIRS_PER_LANE_BLOCK):
            k = c * PAIRS_PER_LANE_BLOCK + q
            parts += [bre_scr[k].astype(BF16), bim_scr[k].astype(BF16)]
        lhs = jnp.concatenate(parts, axis=1)
        y = jnp.dot(lhs, wc_ref[c], preferred_element_type=F32)
        y = y + d_ref[:, c * LANES:(c + 1) * LANES] * utb_scr[c]
        ytb_scr[c] = jax.nn.gelu(y)
        for b in range(batch):
            yg_scr[pl.ds(b * tt, tt), c * LANES:(c + 1) * LANES] = (
                ytb_scr[c, pl.ds(b, tt, stride=batch), :].astype(BF16))

    for c in range(N_LANE_BLOCKS + 2):
        if c < N_LANE_BLOCKS:
            bproj(c)
        if 1 <= c <= N_LANE_BLOCKS:
            scan(c - 1)
        if c >= 2:
            cproj(c - 2)

    yg = yg_scr[...]
    for c in range(D_MODEL // GLU_CHUNK):
        sl = slice(c * GLU_CHUNK, (c + 1) * GLU_CHUNK)
        val = jnp.dot(yg, wglu_ref[:, sl], preferred_element_type=F32)
        gate = jnp.dot(yg, wglu_ref[:, D_MODEL + c * GLU_CHUNK:D_MODEL + (c + 1) * GLU_CHUNK],
                       preferred_element_type=F32)
        out = x_ref[:, :, sl].reshape(rows, GLU_CHUNK) + val * jax.nn.sigmoid(gate)
        o_ref[:, :, sl] = out.reshape(batch, tt, GLU_CHUNK)


def _s5_tables(a_re, a_im, log_dt, b_re, b_im, c_re, c_im, batch):
    dt = jnp.exp(log_dt)[:, None]
    mag = jnp.exp(a_re * dt)
    ab_re = mag * jnp.cos(a_im * dt)
    ab_im = mag * jnp.sin(a_im * dt)
    den = a_re * a_re + a_im * a_im
    f_re = ((ab_re - 1.0) * a_re + ab_im * a_im) / den
    f_im = (ab_im * a_re - (ab_re - 1.0) * a_im) / den
    bb_re = f_re[..., None] * b_re - f_im[..., None] * b_im
    bb_im = f_re[..., None] * b_im + f_im[..., None] * b_re
    groups_per_block = N_GROUPS // N_LANE_BLOCKS
    bb = jnp.stack([bb_re, bb_im]).reshape(2, N_PAIRS, GROUPS_PER_PAIR, STATE, SSM_GROUP)
    pair = jnp.arange(N_PAIRS)[:, None]
    member = jnp.arange(GROUPS_PER_PAIR)[None, :]
    local_group = (GROUPS_PER_PAIR * pair) % groups_per_block + member
    onehot = (local_group[..., None] == jnp.arange(groups_per_block)).astype(F32)
    wb = jnp.einsum('rkjph,kjg->kghrjp', bb, onehot).reshape(N_PAIRS, LANES, 2 * LANES)
    wb = wb.reshape(N_LANE_BLOCKS, PAIRS_PER_LANE_BLOCK, LANES, 2 * LANES).transpose(0, 2, 1, 3)
    wb = wb.reshape(N_LANE_BLOCKS, LANES, PAIRS_PER_LANE_BLOCK * 2 * LANES)
    cc = jnp.stack([c_re, -c_im]).reshape(
        2, N_LANE_BLOCKS, PAIRS_PER_LANE_BLOCK, GROUPS_PER_PAIR, SSM_GROUP, STATE)
    wc = jnp.einsum('rcqjhp,qa,jb->cqrjpabh', cc, jnp.eye(PAIRS_PER_LANE_BLOCK, dtype=F32),
                    jnp.eye(GROUPS_PER_PAIR, dtype=F32)).reshape(N_LANE_BLOCKS, D_MODEL, LANES)
    a_re_t = jnp.broadcast_to(ab_re.reshape(N_PAIRS, 1, LANES), (N_PAIRS, batch, LANES))
    a_im_t = jnp.broadcast_to(ab_im.reshape(N_PAIRS, 1, LANES), (N_PAIRS, batch, LANES))
    return wb.astype(BF16), wc.astype(BF16), a_re_t, a_im_t


def _s5_mixer(x, norm_w, w_in, a_re, a_im, log_dt, b_re, b_im, c_re, c_im, d_skip, w_glu,
              cast_weights=()):
    batch, seq, _ = x.shape
    assert batch == SUBLANES, "the scan keeps one timestep of all batches in one sublane tile"
    tt = S5_TT
    rows = batch * tt
    n_steps = seq // tt
    wb, wc, a_re_t, a_im_t = _s5_tables(a_re, a_im, log_dt, b_re, b_im, c_re, c_im, batch)
    rider_in, rider_out, rider_shapes, rider_args = _cast_riders(cast_weights, n_steps, lambda i: i)
    kern = functools.partial(_s5_kernel, batch=batch, tt=tt, n_riders=len(rider_args))
    out, *cast = pl.pallas_call(
        kern,
        out_shape=[jax.ShapeDtypeStruct(x.shape, F32)] + rider_shapes,
        grid=(n_steps,),
        in_specs=[
            pl.BlockSpec((batch, tt, D_MODEL), lambda i: (0, i, 0)),
            _resident((1, D_MODEL)),
            _resident((D_MODEL, D_MODEL)),
            _resident((N_LANE_BLOCKS, LANES, PAIRS_PER_LANE_BLOCK * 2 * LANES)),
            _resident((N_LANE_BLOCKS, D_MODEL, LANES)),
            _resident((N_PAIRS, batch, LANES)),
            _resident((N_PAIRS, batch, LANES)),
            _resident((1, D_MODEL)),
            _resident((D_MODEL, 2 * D_MODEL)),
        ] + rider_in,
        out_specs=[pl.BlockSpec((batch, tt, D_MODEL), lambda i: (0, i, 0))] + rider_out,
        scratch_shapes=[
            pltpu.VMEM((N_LANE_BLOCKS, rows, LANES), F32),
            pltpu.VMEM((N_PAIRS, rows, LANES), F32),
            pltpu.VMEM((N_PAIRS, rows, LANES), F32),
            pltpu.VMEM((N_LANE_BLOCKS, rows, LANES), F32),
            pltpu.VMEM((rows, D_MODEL), BF16),
            pltpu.VMEM((N_PAIRS, batch, LANES), F32),
            pltpu.VMEM((N_PAIRS, batch, LANES), F32),
        ],
        compiler_params=pltpu.CompilerParams(
            dimension_semantics=("arbitrary",), vmem_limit_bytes=VMEM_LIMIT),
        name="s5_mixer",
    )(x, norm_w.reshape(1, D_MODEL), w_in.astype(BF16), wb, wc, a_re_t, a_im_t,
      d_skip.reshape(1, D_MODEL), w_glu.astype(BF16), *rider_args)
    return out, cast


def _ffn_kernel(*refs, tl, pre_proj, final_norm, n_riders):
    refs = list(refs)
    x_ref = refs.pop(0)
    if pre_proj:
        a_ref, wo_ref = refs.pop(0), refs.pop(0)
    nw_ref, wup_ref, cw_ref, cb_ref, wd_ref = (refs.pop(0) for _ in range(5))
    if final_norm:
        nf_ref = refs.pop(0)
    rider_src = [refs.pop(0) for _ in range(n_riders)]
    o_ref = refs.pop(0)
    rider_dst = [refs.pop(0) for _ in range(n_riders)]
    gs_scr, act_scr = refs
    _run_cast_riders(rider_src, rider_dst)

    @pl.when(pl.program_id(1) == 0)
    def _():
        gs_scr[0:SUBLANES, :] = jnp.zeros((SUBLANES, D_FF), F32)

    x = x_ref[...]
    if pre_proj:
        x = x + jnp.dot(a_ref[...], wo_ref[...], preferred_element_type=F32)
    h = _rms(x, nw_ref[...]).astype(BF16)
    gu = jnp.dot(h, wup_ref[...], preferred_element_type=F32)
    for c in range(D_FF // FF_CHUNK):
        sl = slice(c * FF_CHUNK, (c + 1) * FF_CHUNK)
        g = gu[:, sl]
        up = gu[:, D_FF + c * FF_CHUNK:D_FF + (c + 1) * FF_CHUNK]
        gs_scr[SUBLANES:SUBLANES + tl, sl] = g
        g1 = gs_scr[SUBLANES - 1:SUBLANES - 1 + tl, sl]
        g2 = gs_scr[SUBLANES - 2:SUBLANES - 2 + tl, sl]
        gc = cw_ref[0:1, sl] * g2 + cw_ref[1:2, sl] * g1 + cw_ref[2:3, sl] * g + cb_ref[:, sl]
        act_scr[:, sl] = (gc * jax.nn.sigmoid(gc) * up).astype(BF16)
        gs_scr[0:SUBLANES, sl] = g[tl - SUBLANES:tl]
    y = x + jnp.dot(act_scr[...], wd_ref[...], preferred_element_type=F32)
    if final_norm:
        y = _rms(y, nf_ref[...])
    o_ref[...] = y


def _conv_ffn(x, norm_w, w_up, conv_w, conv_b, w_down, attn=None, w_o=None, final_w=None,
              cast_weights=()):
    batch, seq, _ = x.shape
    tl = FFN_TL
    steps_per_batch = seq // tl
    pre_proj = attn is not None
    final_norm = final_w is not None
    row_spec = pl.BlockSpec((None, tl, D_MODEL), lambda b, i: (b, i, 0))
    in_specs = [row_spec]
    args = [x]
    if pre_proj:
        in_specs += [row_spec, _resident((D_MODEL, D_MODEL))]
        args += [attn, w_o.astype(BF16)]
    in_specs += [_resident((1, D_MODEL)), _resident((D_MODEL, 2 * D_FF)), _resident((3, D_FF)),
                 _resident((1, D_FF)), _resident((D_FF, D_MODEL))]
    args += [norm_w.reshape(1, D_MODEL), w_up.astype(BF16), conv_w.reshape(3, D_FF),
             conv_b.reshape(1, D_FF), w_down.astype(BF16)]
    if final_norm:
        in_specs.append(_resident((1, D_MODEL)))
        args.append(final_w.reshape(1, D_MODEL))
    rider_in, rider_out, rider_shapes, rider_args = _cast_riders(
        cast_weights, batch * steps_per_batch, lambda b, i: b * steps_per_batch + i)
    kern = functools.partial(_ffn_kernel, tl=tl, pre_proj=pre_proj, final_norm=final_norm,
                             n_riders=len(rider_args))
    out, *cast = pl.pallas_call(
        kern,
        out_shape=[jax.ShapeDtypeStruct(x.shape, F32)] + rider_shapes,
        grid=(batch, steps_per_batch),
        in_specs=in_specs + rider_in,
        out_specs=[row_spec] + rider_out,
        scratch_shapes=[
            pltpu.VMEM((SUBLANES + tl, D_FF), F32),
            pltpu.VMEM((tl, D_FF), BF16),
        ],
        compiler_params=pltpu.CompilerParams(
            dimension_semantics=("arbitrary", "arbitrary"), vmem_limit_bytes=VMEM_LIMIT),
        name="conv_ffn",
    )(*args, *rider_args)
    return out, cast


def _qkv_kernel(x_ref, nq_ref, nkv_ref, wq_ref, wkv_ref, q_ref, k_ref, v_ref):
    x = x_ref[...]
    hq = _rms(x, nq_ref[...]).astype(BF16)
    hkv = _rms(x, nkv_ref[...]).astype(BF16)
    q = jnp.dot(hq, wq_ref[...], preferred_element_type=F32) * Q_SCALE
    kv = jnp.dot(hkv, wkv_ref[...], preferred_element_type=F32)
    q_ref[...] = q.astype(BF16)
    k_ref[...] = kv[:, :D_MODEL].astype(BF16)
    v_ref[...] = kv[:, D_MODEL:].astype(BF16)


def _qkv_proj(x, norm_q, norm_kv, w_q, w_kv):
    batch, seq, _ = x.shape
    tl = FFN_TL
    row_spec = pl.BlockSpec((None, tl, D_MODEL), lambda b, i: (b, i, 0))
    out = jax.ShapeDtypeStruct(x.shape, BF16)
    return pl.pallas_call(
        _qkv_kernel,
        out_shape=(out, out, out),
        grid=(batch, seq // tl),
        in_specs=[row_spec, _resident((1, D_MODEL)), _resident((1, D_MODEL)),
                  _resident((D_MODEL, D_MODEL)), _resident((D_MODEL, 2 * D_MODEL))],
        out_specs=(row_spec, row_spec, row_spec),
        compiler_params=pltpu.CompilerParams(
            dimension_semantics=("arbitrary", "arbitrary"), vmem_limit_bytes=VMEM_LIMIT),
        name="qkv_proj",
    )(x, norm_q.reshape(1, D_MODEL), norm_kv.reshape(1, D_MODEL), w_q.astype(BF16),
      w_kv.astype(BF16))


def _softplus2(z):
    return jnp.where(z > SOFTPLUS2_LINEAR_ABOVE, z, jnp.log2(1.0 + jnp.exp2(z)))


def _attn_kernel(q_ref, k_ref, v_ref, tri_ref, o_ref, z_scr, sp_scr, rs_scr, acc_scr, rem_scr,
                 *, bq, bk, n_sub):
    lane = lax.broadcasted_iota(jnp.int32, (bq, LANES), 1)
    first = lane < HEAD_DIM

    acc_scr[...] = jnp.zeros_like(acc_scr)
    rem_scr[...] = jnp.zeros_like(rem_scr)

    def walker(sub):
        qb = pl.program_id(2) * n_sub + sub
        n_blocks = qb + 1
        q2 = q_ref[sub * bq:(sub + 1) * bq, :]
        zero = jnp.zeros_like(q2)
        q_rows = jnp.concatenate([jnp.where(first, q2, zero), jnp.where(first, zero, q2)], axis=0)
        z_sub, sp_sub, rs_sub = z_scr.at[sub], sp_scr.at[sub], rs_scr.at[sub]
        acc_sub, rem_sub = acc_scr.at[sub], rem_scr.at[sub]

        def stage1(j, slot, diagonal=False, valid=None):
            start = pl.multiple_of(j * bk, bk)
            kb = k_ref[pl.ds(start, bk), :]
            if diagonal:
                row = lax.broadcasted_iota(jnp.int32, (bq, bk), 0)
                col = lax.broadcasted_iota(jnp.int32, (bq, bk), 1)
                keep = col < row
            elif valid is not None:
                keep = valid
            else:
                keep = None
            if diagonal:
                keep = jnp.concatenate([keep, keep], axis=0)
            z = lax.dot_general(q_rows, kb, (((1,), (1,)), ((), ())), preferred_element_type=F32)
            if keep is not None:
                z = jnp.where(keep, z, MASKED_SCORE)
            sp = _softplus2(z)
            z_sub[slot] = z.reshape(2, bq, bk)
            sp_sub[slot] = sp.astype(BF16).reshape(2, bq, bk)
            rs_sub[slot] = jnp.broadcast_to(
                jnp.sum(sp, axis=-1, keepdims=True), (2 * bq, LANES)).reshape(2, bq, LANES)

        def stage2(j, slot):
            start = pl.multiple_of(j * bk, bk)
            vb = v_ref[pl.ds(start, bk), :]
            cs = jnp.dot(sp_sub[slot].reshape(2 * bq, bk), tri_ref[...],
                         preferred_element_type=F32)
            rem = rem_sub[...].reshape(2 * bq, LANES)
            logw = jnp.minimum(z_sub[slot].reshape(2 * bq, bk) - cs, 0.0)
            w = jnp.exp2(logw - jnp.concatenate([rem] * (bk // LANES), axis=1))
            pv = jnp.dot(w.astype(BF16), vb, preferred_element_type=F32)
            acc_sub[...] += pv.reshape(2, bq, LANES)
            rem_sub[...] = (rem + rs_sub[slot].reshape(2 * bq, LANES)).reshape(2, bq, LANES)

        def blocks_beyond_are_dead(slot=None):
            r = rem_sub[...]
            if slot is not None:
                r = r + rs_sub[slot]
            return jnp.min(r) >= LOG2_W_UNDERFLOW

        def first_part():
            j1 = jnp.maximum(n_blocks - 2, 0)
            stage1(n_blocks - 1, 0, diagonal=True)
            stage2(n_blocks - 1, 0)
            stage1(j1, 1, valid=n_blocks > 1)
            stage2(j1, 1)
            return jnp.logical_and(n_blocks > 2, jnp.logical_not(blocks_beyond_are_dead()))

        def rest(more_blocks_live):
            @pl.when(more_blocks_live)
            def _():
                stage1(n_blocks - 3, 0)

                def cond(carry):
                    p, dead = carry
                    return jnp.logical_and(p < n_blocks - 1, jnp.logical_not(dead))

                def body(carry):
                    p, _ = carry
                    slot = p % 2
                    stage2(n_blocks - 1 - p, slot)
                    stage1(n_blocks - 2 - p, 1 - slot)
                    return p + 1, blocks_beyond_are_dead(slot=1 - slot)

                p_last, _ = lax.while_loop(cond, body,
                                           (jnp.int32(2), blocks_beyond_are_dead(slot=0)))
                stage2(n_blocks - 1 - p_last, p_last % 2)

        return first_part, rest

    walkers = [walker(sub) for sub in range(n_sub)]
    live = [first_part() for first_part, _ in walkers]
    for (_, rest), more_blocks_live in zip(walkers, live):
        rest(more_blocks_live)
    for sub in range(n_sub):
        o_ref[sub * bq:(sub + 1) * bq, :] = jnp.where(
            first, acc_scr[sub, 0], acc_scr[sub, 1]).astype(BF16)


def _sb_attention(q, k, v):
    batch, seq, _ = q.shape
    bq, bk, n_sub = ATT_BQ, ATT_BK, ATT_Q_BLOCKS_PER_STEP
    assert bq == bk, "exactly the first key block touches the diagonal"
    tri = (jnp.arange(bk)[:, None] >= jnp.arange(bk)[None, :]).astype(BF16)
    kern = functools.partial(_attn_kernel, bq=bq, bk=bk, n_sub=n_sub)
    kv_spec = pl.BlockSpec((None, seq, LANES), lambda b, hp, i: (b, 0, hp))
    q_spec = pl.BlockSpec((None, n_sub * bq, LANES), lambda b, hp, i: (b, i, hp))
    return pl.pallas_call(
        kern,
        out_shape=jax.ShapeDtypeStruct(q.shape, BF16),
        grid=(batch, N_HEADS * HEAD_DIM // LANES, seq // (n_sub * bq)),
        in_specs=[q_spec, kv_spec, kv_spec, _resident((bk, bk))],
        out_specs=q_spec,
        scratch_shapes=[
            pltpu.VMEM((n_sub, 2, 2, bq, bk), F32),
            pltpu.VMEM((n_sub, 2, 2, bq, bk), BF16),
            pltpu.VMEM((n_sub, 2, 2, bq, LANES), F32),
            pltpu.VMEM((n_sub, 2, bq, LANES), F32),
            pltpu.VMEM((n_sub, 2, bq, LANES), F32),
        ],
        compiler_params=pltpu.CompilerParams(
            dimension_semantics=("arbitrary", "arbitrary", "arbitrary"),
            vmem_limit_bytes=VMEM_LIMIT),
        name="sb_attn",
    )(q, k, v, tri)


def kernel(x, norm_mix, norm_ffn, norm_kv, norm_final, ssm_w_in, ssm_a_re, ssm_a_im, ssm_log_dt, ssm_b_re, ssm_b_im, ssm_c_re, ssm_c_im, ssm_d, ssm_w_glu, kv_w, attn_w_q, attn_w_o, ffn_w_up, ffn_conv_w, ffn_conv_b, ffn_w_down):
    depth = norm_mix.shape[0]
    n_a = ssm_w_in.shape[0]
    assert depth == 2 and n_a == 1 and attn_w_q.shape[0] == 1
    x, (w_up0, w_down0, w_q, w_kv) = _s5_mixer(
        x, norm_mix[0], ssm_w_in[0], ssm_a_re[0], ssm_a_im[0], ssm_log_dt[0], ssm_b_re[0],
        ssm_b_im[0], ssm_c_re[0], ssm_c_im[0], ssm_d[0], ssm_w_glu[0],
        cast_weights=[(ffn_w_up, 0), (ffn_w_down, 0), (attn_w_q, 0), (kv_w[None], 0)])
    x, (w_up1, w_down1, w_o) = _conv_ffn(
        x, norm_ffn[0], w_up0, ffn_conv_w[0], ffn_conv_b[0], w_down0,
        cast_weights=[(ffn_w_up, 1), (ffn_w_down, 1), (attn_w_o, 0)])
    q, k, v = _qkv_proj(x, norm_mix[1], norm_kv, w_q, w_kv)
    attn = _sb_attention(q, k, v)
    out, _ = _conv_ffn(x, norm_ffn[1], w_up1, ffn_conv_w[1], ffn_conv_b[1], w_down1,
                       attn=attn, w_o=w_o, final_w=norm_final)
    return out
```
